```python
import math
import jax, jax.numpy as jnp
from jax import lax
import numpy as np

D_MODEL = 1024
BATCH = 1
SEQ = 16384
DEPTH = 2

EPS = 1e-6

SSD_EXPAND = 2
SSD_INNER = SSD_EXPAND * D_MODEL
SSD_HEAD_DIM = 64
SSD_HEADS = SSD_INNER // SSD_HEAD_DIM
SSD_GROUPS = 4
SSD_STATE = 128
SSD_CONV = 5
SSD_CHUNK = 128
SSD_CONV_DIM = SSD_INNER + 2 * SSD_GROUPS * SSD_STATE

ATT_HEADS = 16
ATT_KV_HEADS = 4
ATT_HEAD_DIM = 64
ATT_WIDTH = ATT_HEADS * ATT_HEAD_DIM
ATT_KV_WIDTH = ATT_KV_HEADS * ATT_HEAD_DIM
Q_BLOCK = 128
ROPE_THETA = 10000.0
GRID_W = 64

POOL_WIDTH = D_MODEL
POOL_WINDOWS = (2, 4, 8, 16)
POOL_GROUPS = 4
POOL_GROUP_DIM = POOL_WIDTH // POOL_GROUPS

N_BRANCH = 3

IN_SIZES = (SSD_INNER, SSD_CONV_DIM, 2 * SSD_HEADS, ATT_WIDTH, 2 * ATT_KV_WIDTH, POOL_WIDTH, N_BRANCH * D_MODEL)
IN_DIM = sum(IN_SIZES)

MOE_GROUPS = 4
MOE_PER_GROUP = 8
N_EXPERTS = MOE_GROUPS * MOE_PER_GROUP
MOE_TOP_K = 2
EXPERT_FF = 512

kernel_name = "hybrid_ssd_gqa_pool_hmoe_encoder"


def _split_points(sizes):
    pts, acc = [], 0
    for sz in sizes[:-1]:
        acc += sz
        pts.append(acc)
    return pts


def rms_norm(x, g):
    xf = x.astype(jnp.float32)
    y = xf * lax.rsqrt(jnp.mean(xf * xf, axis=-1, keepdims=True) + EPS)
    return (y * g.astype(jnp.float32)).astype(x.dtype)


def centred_depthwise_conv(u, w, bias):
    ch = u.shape[-1]
    pad = SSD_CONV // 2
    y = lax.conv_general_dilated(u, w[:, None, :], window_strides=(1,), padding=((pad, pad),),
                                 dimension_numbers=('NWC', 'WIO', 'NWC'), feature_group_count=ch)
    return y + bias


def ssd_chunked(xh, dt, a, bm, cm):
    f32 = jnp.float32
    b, s, h, p = xh.shape
    g, n = bm.shape[-2:]
    r = h // g
    q = SSD_CHUNK
    c = s // q
    xdt = (xh.astype(f32) * dt[..., None]).reshape(b, c, q, g, r, p)
    bm = bm.astype(f32).reshape(b, c, q, g, n)
    cm = cm.astype(f32).reshape(b, c, q, g, n)
    da_cs = jnp.cumsum((dt * a).reshape(b, c, q, g, r), axis=2)
    lower_tri = jnp.tril(jnp.ones((q, q), dtype=bool))[None, None, :, :, None, None]
    seg = da_cs[:, :, :, None] - da_cs[:, :, None, :]
    decay = jnp.exp(jnp.where(lower_tri, seg, -jnp.inf))
    cb = jnp.einsum('bclgn,bcsgn->bclsg', cm, bm)
    y_diag = jnp.einsum('bclsg,bclsgr,bcsgrp->bclgrp', cb, decay, xdt)
    decay_to_end = jnp.exp(da_cs[:, :, -1:] - da_cs)
    chunk_states = jnp.einsum('bcsgn,bcsgr,bcsgrp->bcgrpn', bm, decay_to_end, xdt)
    chunk_decay = jnp.exp(da_cs[:, :, -1])

    def carry_state(state, inputs):
        st, dec = inputs
        return state * dec[..., None, None] + st, state

    init = jnp.zeros((b, g, r, p, n), f32)
    _, prev = lax.scan(carry_state, init, (jnp.moveaxis(chunk_states, 1, 0), jnp.moveaxis(chunk_decay, 1, 0)))
    prev = jnp.moveaxis(prev, 0, 1)
    y_off = jnp.einsum('bclgn,bcgrpn,bclgr->bclgrp', cm, prev, jnp.exp(da_cs))
    return (y_diag + y_off).reshape(b, s, h, p)


def axial_rope_tables(seq_len):
    rows = seq_len // GRID_W
    row = jnp.repeat(jnp.arange(rows), GRID_W).astype(jnp.float32)
    col = jnp.tile(jnp.arange(GRID_W), rows).astype(jnp.float32)
    half = ATT_HEAD_DIM // 2
    inv_freq = ROPE_THETA ** (-jnp.arange(0, half, 2, dtype=jnp.float32) / half)
    ang = jnp.concatenate([row[:, None] * inv_freq, col[:, None] * inv_freq], axis=-1)
    return jnp.cos(ang), jnp.sin(ang)


def apply_axial_rope(x, cos, sin):
    b, s, h, d = x.shape
    xr = x.reshape(b, s, h, 2, 2, d // 4)
    x1, x2 = xr[..., 0, :], xr[..., 1, :]
    c = cos.reshape(s, 2, d // 4)[None, :, None]
    sn = sin.reshape(s, 2, d // 4)[None, :, None]
    out = jnp.stack([x1 * c - x2 * sn, x2 * c + x1 * sn], axis=-2)
    return out.reshape(b, s, h, d).astype(x.dtype)


def block_attention(q, k, v):
    b, s, h, d = q.shape
    hk = k.shape[2]
    r = h // hk
    nb = s // Q_BLOCK
    qb = jnp.moveaxis((q * (d ** -0.5)).reshape(b, nb, Q_BLOCK, hk, r, d), 1, 0)

    def one_block(qi):
        scores = jnp.einsum('bqkrd,bskd->bkrqs', qi, k).astype(jnp.float32)
        probs = jax.nn.softmax(scores, axis=-1).astype(v.dtype)
        return jnp.einsum('bkrqs,bskd->bqkrd', probs, v)

    out = lax.map(one_block, qb)
    return jnp.moveaxis(out, 0, 1).reshape(b, s, h * d)


def multiscale_pool(u, w_pool, pool_scale):
    b, s, _ = u.shape
    ug = u.reshape(b, s, POOL_GROUPS, POOL_GROUP_DIM).astype(jnp.float32)
    cs = jnp.pad(jnp.cumsum(ug, axis=1), ((0, 0), (1, 0), (0, 0), (0, 0)))
    t = jnp.arange(s)[:, None]
    half = jnp.array(POOL_WINDOWS)[None, :] // 2
    lo = jnp.clip(t - half, 0, s)
    hi = jnp.clip(t + half, 0, s)
    g_idx = jnp.arange(POOL_GROUPS)[None, :]
    window_mean = (cs[:, hi, g_idx] - cs[:, lo, g_idx]) / (hi - lo).astype(jnp.float32)[None, :, :, None]
    mixed = (window_mean - ug).astype(u.dtype)
    mixed = jnp.einsum('bsgc,gce->bsge', mixed, w_pool).reshape(b, s, POOL_WIDTH)
    return mixed * pool_scale


def hybrid_mixer(x, norm_mix, w_in, conv_w, conv_b, dt_bias, a_log, d_skip, ssd_norm, w_br_ssd,
                 q_norm, k_norm, w_br_att, w_pool, pool_scale, w_br_pool, gate_bias, w_out, cos, sin):
    b, s, _ = x.shape
    h = rms_norm(x, norm_mix)
    proj = h @ w_in
    z, xbc, dt_raw, q, kv, u_pool, gate_logits = jnp.split(proj, _split_points(IN_SIZES), axis=-1)

    xbc = jax.nn.silu(centred_depthwise_conv(xbc, conv_w, conv_b))
    xs, bm, cm = jnp.split(xbc, [SSD_INNER, SSD_INNER + SSD_GROUPS * SSD_STATE], axis=-1)
    xs = xs.reshape(b, s, SSD_HEADS, SSD_HEAD_DIM)
    bm = bm.reshape(b, s, SSD_GROUPS, SSD_STATE)
    cm = cm.reshape(b, s, SSD_GROUPS, SSD_STATE)
    dt = jax.nn.softplus(dt_raw.astype(jnp.float32).reshape(b, s, 2, SSD_HEADS) + dt_bias.astype(jnp.float32))
    a = -jnp.exp(a_log.astype(jnp.float32))
    rev = lambda t_: jnp.flip(t_, axis=1)
    y_fwd = ssd_chunked(xs, dt[:, :, 0], a[0], bm, cm)
    y_bwd = rev(ssd_chunked(rev(xs), rev(dt[:, :, 1]), a[1], rev(bm), rev(cm)))
    y = (y_fwd + y_bwd).astype(x.dtype) + xs * d_skip[:, None]
    y = rms_norm(y.reshape(b, s, SSD_INNER) * jax.nn.silu(z), ssd_norm)
    branch_ssd = y @ w_br_ssd

    k, v = jnp.split(kv, 2, axis=-1)
    q = apply_axial_rope(rms_norm(q.reshape(b, s, ATT_HEADS, ATT_HEAD_DIM), q_norm), cos, sin)
    k = apply_axial_rope(rms_norm(k.reshape(b, s, ATT_KV_HEADS, ATT_HEAD_DIM), k_norm), cos, sin)
    v = v.reshape(b, s, ATT_KV_HEADS, ATT_HEAD_DIM)
    branch_att = block_attention(q, k, v) @ w_br_att

    branch_pool = multiscale_pool(u_pool, w_pool, pool_scale) @ w_br_pool

    gates = jax.nn.sigmoid(gate_logits.reshape(b, s, N_BRANCH, D_MODEL) + gate_bias)
    merged = gates[:, :, 0] * branch_ssd + gates[:, :, 1] * branch_att + gates[:, :, 2] * branch_pool
    return merged @ w_out


def hierarchical_moe(x, norm_ffn, w_router_group, b_router_group, w_router_expert, b_router_expert,
                     w_gate_e, w_up_e, w_down_e):
    b, s, d = x.shape
    n_tok = b * s
    h = rms_norm(x, norm_ffn).reshape(n_tok, d)
    g_prob = jax.nn.softmax((h @ w_router_group).astype(jnp.float32) + b_router_group, axis=-1)
    p_group, g_sel = lax.top_k(g_prob, 1)
    e_logits = ((h @ w_router_expert).astype(jnp.float32) + b_router_expert).reshape(n_tok, MOE_GROUPS, MOE_PER_GROUP)
    e_logits = jnp.take_along_axis(e_logits, g_sel[:, :, None], axis=1)[:, 0]
    p_exp, e_sel = lax.top_k(jax.nn.softmax(e_logits, axis=-1), MOE_TOP_K)
    p_exp = p_exp / jnp.sum(p_exp, axis=-1, keepdims=True)
    weights = (p_group * p_exp).astype(x.dtype)
    expert_idx = g_sel * MOE_PER_GROUP + e_sel
    combine = jnp.zeros((n_tok, N_EXPERTS), x.dtype).at[jnp.arange(n_tok)[:, None], expert_idx].add(weights)
    out = jnp.zeros_like(h)
    for e in range(N_EXPERTS):
        hid = jax.nn.silu(h @ w_gate_e[e]) * (h @ w_up_e[e])
        out = out + combine[:, e:e + 1] * (hid @ w_down_e[e])
    return out.reshape(b, s, d)


def setup_inputs(seed: int = 0) -> dict:
    key = jax.random.key(seed)
    ks = jax.random.split(key, 26)
    f32 = jnp.float32
    L = DEPTH

    def nrm(k, shape, fan_in):
        return jax.random.normal(k, shape, f32) * (fan_in ** -0.5)

    def gain(k, shape):
        return 1.0 + 0.05 * jax.random.normal(k, shape, f32)

    def small(k, shape, scale=0.02):
        return scale * jax.random.normal(k, shape, f32)

    dt0 = jnp.exp(jax.random.uniform(ks[5], (L, 2, SSD_HEADS), f32, math.log(1e-3), math.log(1e-1)))
    return {
        'x': jax.random.normal(ks[0], (BATCH, SEQ, D_MODEL), f32),
        'norm_mix': gain(ks[1], (L, D_MODEL)),
        'w_in': nrm(ks[2], (L, D_MODEL, IN_DIM), D_MODEL),
        'conv_w': nrm(ks[3], (L, SSD_CONV, SSD_CONV_DIM), SSD_CONV),
        'conv_b': small(ks[4], (L, SSD_CONV_DIM)),
        'dt_bias': dt0 + jnp.log(-jnp.expm1(-dt0)),
        'a_log': jnp.log(jax.random.uniform(ks[6], (L, 2, SSD_HEADS), f32, 1.0, 16.0)),
        'd_skip': gain(ks[7], (L, SSD_HEADS)),
        'ssd_norm': gain(ks[8], (L, SSD_INNER)),
        'w_br_ssd': nrm(ks[9], (L, SSD_INNER, D_MODEL), SSD_INNER),
        'q_norm': gain(ks[10], (L, ATT_HEAD_DIM)),
        'k_norm': gain(ks[11], (L, ATT_HEAD_DIM)),
        'w_br_att': nrm(ks[12], (L, ATT_WIDTH, D_MODEL), ATT_WIDTH),
        'w_pool': nrm(ks[13], (L, POOL_GROUPS, POOL_GROUP_DIM, POOL_GROUP_DIM), POOL_GROUP_DIM),
        'pool_scale': gain(ks[14], (L, POOL_WIDTH)),
        'w_br_pool': nrm(ks[15], (L, POOL_WIDTH, D_MODEL), POOL_WIDTH),
        'gate_bias': small(ks[16], (L, N_BRANCH, D_MODEL)),
        'w_out': nrm(ks[17], (L, D_MODEL, D_MODEL), D_MODEL),
        'norm_ffn': gain(ks[18], (L, D_MODEL)),
        'w_router_group': nrm(ks[19], (L, D_MODEL, MOE_GROUPS), D_MODEL),
        'b_router_group': small(ks[20], (L, MOE_GROUPS), 0.01),
        'w_router_expert': nrm(ks[21], (L, D_MODEL, N_EXPERTS), D_MODEL),
        'b_router_expert': small(ks[22], (L, N_EXPERTS), 0.01),
        'w_gate_e': nrm(ks[23], (L, N_EXPERTS, D_MODEL, EXPERT_FF), D_MODEL),
        'w_up_e': nrm(ks[24], (L, N_EXPERTS, D_MODEL, EXPERT_FF), D_MODEL),
        'w_down_e': nrm(ks[25], (L, N_EXPERTS, EXPERT_FF, D_MODEL), EXPERT_FF),
    }


def reference(x, norm_mix, w_in, conv_w, conv_b, dt_bias, a_log, d_skip, ssd_norm, w_br_ssd,
              q_norm, k_norm, w_br_att, w_pool, pool_scale, w_br_pool, gate_bias, w_out,
              norm_ffn, w_router_group, b_router_group, w_router_expert, b_router_expert,
              w_gate_e, w_up_e, w_down_e):
    cos, sin = axial_rope_tables(x.shape[1])
    for l in range(DEPTH):
        x = x + hybrid_mixer(x, norm_mix[l], w_in[l], conv_w[l], conv_b[l], dt_bias[l], a_log[l],
                             d_skip[l], ssd_norm[l], w_br_ssd[l], q_norm[l], k_norm[l], w_br_att[l],
                             w_pool[l], pool_scale[l], w_br_pool[l], gate_bias[l], w_out[l], cos, sin)
        x = x + hierarchical_moe(x, norm_ffn[l], w_router_group[l], b_router_group[l],
                                 w_router_expert[l], b_router_expert[l], w_gate_e[l], w_up_e[l], w_down_e[l])
    return x
```

```python
import functools
import math

import jax
import jax.numpy as jnp
from jax import lax
from jax.experimental import pallas as pl
from jax.experimental.pallas import tpu as pltpu

F32 = jnp.float32
BF16 = jnp.bfloat16
I32 = jnp.int32
HIGHEST = lax.Precision.HIGHEST

EPS = 1e-6
D_MODEL = 1024
DEPTH = 2

SSD_INNER = 2048
SSD_HEAD_DIM = 64
SSD_HEADS = 32
SSD_GROUPS = 4
SSD_STATE = 128
SSD_CONV = 5
SSD_CHUNK = 128
SSD_BC = SSD_GROUPS * SSD_STATE
SSD_CONV_DIM = SSD_INNER + 2 * SSD_BC
HEADS_PER_GROUP = SSD_HEADS // SSD_GROUPS
GROUP_WIDTH = HEADS_PER_GROUP * SSD_HEAD_DIM

ATT_HEADS = 16
ATT_KV_HEADS = 4
ATT_HEAD_DIM = 64
ATT_WIDTH = 1024
ATT_KV_WIDTH = 256
ATT_REP = ATT_HEADS // ATT_KV_HEADS
ROPE_THETA = 10000.0
GRID_W = 64

POOL_WIDTH = 1024
POOL_WINDOWS = (2, 4, 8, 16)
POOL_GROUP_DIM = 256

N_BRANCH = 3
MOE_GROUPS = 4
MOE_PER_GROUP = 8
N_EXPERTS = 32
EXPERT_FF = 512

LANES = 128
SUBLANES = 8
VMEM_LIMIT_BYTES = 56 * 1024 * 1024

COL_Z = 0
COL_XBC = COL_Z + SSD_INNER
COL_GATES = COL_XBC + SSD_CONV_DIM
COL_Q = COL_GATES + N_BRANCH * D_MODEL
COL_POOL = COL_Q + ATT_WIDTH
COL_K = COL_POOL + POOL_WIDTH
COL_V = COL_K + ATT_KV_WIDTH
PROJ_DIM = COL_V + ATT_KV_WIDTH
DT_PAD = LANES

HALO = SUBLANES


def _params(*sem):
    return pltpu.CompilerParams(dimension_semantics=sem, vmem_limit_bytes=VMEM_LIMIT_BYTES)


def _silu(v):
    return v * jax.nn.sigmoid(v)


def _split_bf16(v):
    hi = v.astype(BF16)
    lo = (v - hi.astype(F32)).astype(BF16)
    return hi, lo


def _inproj_kernel(x_ref, g_ref, w_ref, wdt_ref, o_ref, dt_ref, h_ref):
    @pl.when(pl.program_id(1) == 0)
    def _():
        x = x_ref[...]
        ms = jnp.mean(x * x, axis=-1, keepdims=True)
        h = (x * lax.rsqrt(ms + EPS) * g_ref[...]).astype(BF16)
        h_ref[...] = h
        dt_ref[...] = jnp.dot(h, wdt_ref[...], preferred_element_type=F32)

    o_ref[...] = jnp.dot(h_ref[...], w_ref[...], preferred_element_type=F32)


def _in_proj(x, g, w, wdt, tm, tn):
    s = x.shape[0]
    return pl.pallas_call(
        _inproj_kernel,
        grid=(s // tm, PROJ_DIM // tn),
        in_specs=[
            pl.BlockSpec((tm, D_MODEL), lambda i, j: (i, 0)),
            pl.BlockSpec((1, D_MODEL), lambda i, j: (0, 0)),
            pl.BlockSpec((D_MODEL, tn), lambda i, j: (0, j)),
            pl.BlockSpec((D_MODEL, DT_PAD), lambda i, j: (0, 0)),
        ],
        out_specs=[
            pl.BlockSpec((tm, tn), lambda i, j: (i, j)),
            pl.BlockSpec((tm, DT_PAD), lambda i, j: (i, 0)),
        ],
        out_shape=[jax.ShapeDtypeStruct((s, PROJ_DIM), F32), jax.ShapeDtypeStruct((s, DT_PAD), F32)],
        scratch_shapes=[pltpu.VMEM((tm, D_MODEL), BF16)],
        compiler_params=_params("parallel", "arbitrary"),
        name="in_proj",
    )(x, g, w, wdt)


def _fill_ext(ext_ref, main_ref, prev_ref, next_ref, tm):
    i = pl.program_id(0)
    last = pl.num_programs(0) - 1
    ext_ref[0:HALO, :] = jnp.where(i > 0, prev_ref[...], 0.0)
    ext_ref[HALO:HALO + tm, :] = main_ref[...]
    ext_ref[HALO + tm:, :] = jnp.where(i < last, next_ref[...], 0.0)


def _conv_kernel(xm_ref, xp_ref, xn_ref, w_ref, b_ref, o_ref, ext_ref):
    tm = xm_ref.shape[0]
    _fill_ext(ext_ref, xm_ref, xp_ref, xn_ref, tm)
    pad = SSD_CONV // 2
    acc = jnp.broadcast_to(b_ref[...], o_ref.shape)
    for k in range(SSD_CONV):
        acc = acc + ext_ref[pl.ds(HALO - pad + k, tm), :] * w_ref[k:k + 1, :]
    o_ref[...] = _silu(acc)


def _halo_specs(tm, tc, col0, s):
    rb = tm // HALO
    nrb = s // HALO
    return [
        pl.BlockSpec((tm, tc), lambda i, j=0: (i, col0 + j)),
        pl.BlockSpec((HALO, tc), lambda i, j=0: (jnp.maximum(i * rb - 1, 0), col0 + j)),
        pl.BlockSpec((HALO, tc), lambda i, j=0: (jnp.minimum((i + 1) * rb, nrb - 1), col0 + j)),
    ]


def _conv(proj, w, b, tm):
    s = proj.shape[0]
    tc = 1024
    return pl.pallas_call(
        _conv_kernel,
        grid=(s // tm, SSD_CONV_DIM // tc),
        in_specs=_halo_specs(tm, tc, COL_XBC // tc, s) + [
            pl.BlockSpec((SSD_CONV, tc), lambda i, j: (0, j)),
            pl.BlockSpec((1, tc), lambda i, j: (0, j)),
        ],
        out_specs=pl.BlockSpec((tm, tc), lambda i, j: (i, j)),
        out_shape=jax.ShapeDtypeStruct((s, SSD_CONV_DIM), F32),
        scratch_shapes=[pltpu.VMEM((tm + 2 * HALO, tc), F32)],
        compiler_params=_params("parallel", "parallel"),
        name="conv",
    )(proj, proj, proj, w, b)


def _ssd_direction(x_ref, b_ref, c_ref, dt_ref, e_ref, y_ref, st_ref, bias, a, fwd):
    q = SSD_CHUNK
    row = lax.broadcasted_iota(I32, (q, q), 0)
    col = lax.broadcasted_iota(I32, (q, q), 1)
    mask = (col <= row) if fwd else (col >= row)
    lane0 = 0 if fwd else SSD_HEADS

    dt = jax.nn.softplus(dt_ref[...] + bias)
    da = dt * a
    cs = jnp.dot(mask.astype(F32), da, precision=HIGHEST, preferred_element_type=F32)
    cs_t = cs.T
    edge = q - 1 if fwd else 0
    tot = cs[edge:edge + 1, :]
    dt_hi, dt_lo = _split_bf16(dt)
    dec = jnp.exp(tot)
    dec_hi = dec.astype(BF16).astype(F32)
    r16 = lax.broadcasted_iota(I32, (2 * SUBLANES, LANES), 0)
    dec_rows = jnp.where(r16 == 0, dec_hi, jnp.where(r16 == 1, dec - dec_hi, 0.0)).astype(BF16)
    lhs = jnp.concatenate(
        [dt_hi, dt_lo, jnp.exp(cs).astype(BF16), jnp.exp(tot - cs).astype(BF16), dec_rows], axis=0)
    ex = jnp.dot(lhs, e_ref[...], preferred_element_type=F32)
    dt_x = ex[0:q] + ex[q:2 * q]
    ecs_x = ex[2 * q:3 * q]
    dte_x = ex[3 * q:4 * q]
    dec_x = ex[4 * q:4 * q + 1] + ex[4 * q + 1:4 * q + 2]

    xdt = x_ref[...] * dt_x
    xdte_b = (xdt * dte_x).astype(BF16)
    lane = lax.broadcasted_iota(I32, (q, LANES), 1)
    first_half = lane < SSD_HEAD_DIM
    bm = b_ref[...].astype(BF16)
    cm = c_ref[...].astype(BF16)
    for g in range(SSD_GROUPS):
        bg = bm[:, g * SSD_STATE:(g + 1) * SSD_STATE]
        cg = cm[:, g * SSD_STATE:(g + 1) * SSD_STATE]
        cb = lax.dot_general(cg, bg, (((1,), (1,)), ((), ())), preferred_element_type=F32)
        gs = slice(g * GROUP_WIDTH, (g + 1) * GROUP_WIDTH)
        st = st_ref[g]
        y_off = jnp.dot(cg, st.astype(BF16), preferred_element_type=F32) * ecs_x[:, gs]
        for pr in range(HEADS_PER_GROUP // 2):
            c0 = g * GROUP_WIDTH + pr * LANES
            xp = xdt[:, c0:c0 + LANES]
            halves = (jnp.where(first_half, xp, 0.0).astype(BF16), jnp.where(first_half, 0.0, xp).astype(BF16))
            yd = None
            for k in range(2):
                hl = lane0 + g * HEADS_PER_GROUP + 2 * pr + k
                seg = cs[:, hl:hl + 1] - cs_t[hl:hl + 1, :]
                decay = jnp.exp(jnp.where(mask, seg, -jnp.inf))
                part = jnp.dot((cb * decay).astype(BF16), halves[k], preferred_element_type=F32)
                yd = part if yd is None else yd + part
            y_ref[:, c0:c0 + LANES] = yd + y_off[:, pr * LANES:(pr + 1) * LANES]
        upd = lax.dot_general(bg, xdte_b[:, gs], (((0,), (0,)), ((), ())), preferred_element_type=F32)
        st_ref[g] = st * dec_x[:, gs] + upd


def _ssd_kernel(xf_ref, bf_ref, cf_ref, dtf_ref, xb_ref, bb_ref, cb_ref, dtb_ref,
                bias_ref, alog_ref, ef_ref, eb_ref, yf_ref, yb_ref, stf_ref, stb_ref):
    @pl.when(pl.program_id(0) == 0)
    def _():
        stf_ref[...] = jnp.zeros_like(stf_ref)
        stb_ref[...] = jnp.zeros_like(stb_ref)

    bias = bias_ref[...]
    a = -jnp.exp(alog_ref[...])
    _ssd_direction(xf_ref, bf_ref, cf_ref, dtf_ref, ef_ref, yf_ref, stf_ref, bias, a, True)
    _ssd_direction(xb_ref, bb_ref, cb_ref, dtb_ref, eb_ref, yb_ref, stb_ref, bias, a, False)


def _ssd(xbc, dt_raw, dt_bias, a_log, e_f, e_b):
    s = xbc.shape[0]
    q = SSD_CHUNK
    nc = s // q
    nb = SSD_INNER // SSD_BC

    def chunk_specs(im):
        return [
            pl.BlockSpec((q, SSD_INNER), lambda i: (im(i), 0)),
            pl.BlockSpec((q, SSD_BC), lambda i: (im(i), nb)),
            pl.BlockSpec((q, SSD_BC), lambda i: (im(i), nb + 1)),
            pl.BlockSpec((q, DT_PAD), lambda i: (im(i), 0)),
        ]

    fwd = lambda i: i
    bwd = lambda i: nc - 1 - i
    const = lambda shape: pl.BlockSpec(shape, lambda i: (0,) * len(shape))
    st_shape = (SSD_GROUPS, SSD_STATE, GROUP_WIDTH)
    return pl.pallas_call(
        _ssd_kernel,
        grid=(nc,),
        in_specs=chunk_specs(fwd) + chunk_specs(bwd) + [
            const((1, DT_PAD)), const((1, DT_PAD)), const((LANES, SSD_INNER)), const((LANES, SSD_INNER))],
        out_specs=[pl.BlockSpec((q, SSD_INNER), lambda i: (fwd(i), 0)),
                   pl.BlockSpec((q, SSD_INNER), lambda i: (bwd(i), 0))],
        out_shape=[jax.ShapeDtypeStruct((s, SSD_INNER), F32)] * 2,
        scratch_shapes=[pltpu.VMEM(st_shape, F32), pltpu.VMEM(st_shape, F32)],
        compiler_params=_params("arbitrary"),
        name="ssd",
    )(xbc, xbc, xbc, dt_raw, xbc, xbc, xbc, dt_raw, dt_bias, a_log, e_f, e_b)


def _norm_rope(x, cos, sin, g, bd, perm):
    hi, lo = _split_bf16(x * x)
    ss = jnp.dot(hi, bd, preferred_element_type=F32) + jnp.dot(lo, bd, preferred_element_type=F32)
    y = x * lax.rsqrt(ss * (1.0 / ATT_HEAD_DIM) + EPS) * g
    yh, yl = _split_bf16(y)
    rot = jnp.dot(yh, perm, preferred_element_type=F32) + jnp.dot(yl, perm, preferred_element_type=F32)
    return y * cos + rot * sin


def _qprep_kernel(x_ref, cos_ref, sin_ref, g_ref, bd_ref, perm_ref, o_ref, *, scale):
    out = _norm_rope(x_ref[...], cos_ref[...], sin_ref[...], g_ref[...], bd_ref[...], perm_ref[...]) * scale
    o_ref[...] = out.T.astype(BF16)


def _kprep_kernel(x_ref, cos_ref, sin_ref, g_ref, bd_ref, perm_ref, o_ref):
    out = _norm_rope(x_ref[...], cos_ref[...], sin_ref[...], g_ref[...], bd_ref[...], perm_ref[...]).astype(BF16)
    o_ref[0] = out[:, :ATT_HEAD_DIM]
    o_ref[1] = out[:, ATT_HEAD_DIM:]


def _vt_kernel(x_ref, o_ref):
    o_ref[...] = x_ref[...].T.astype(BF16)


def _rope_specs(tm):
    return [
        pl.BlockSpec((tm, LANES), lambda i, j: (i, 0)),
        pl.BlockSpec((tm, LANES), lambda i, j: (i, 0)),
        pl.BlockSpec((1, LANES), lambda i, j: (0, 0)),
        pl.BlockSpec((LANES, LANES), lambda i, j: (0, 0)),
        pl.BlockSpec((LANES, LANES), lambda i, j: (0, 0)),
    ]


def _q_prep(proj, cos, sin, g, bd, perm, tm):
    s = proj.shape[0]
    scale = ATT_HEAD_DIM ** -0.5 * math.log2(math.e)
    return pl.pallas_call(
        functools.partial(_qprep_kernel, scale=scale),
        grid=(s // tm, ATT_WIDTH // LANES),
        in_specs=[pl.BlockSpec((tm, LANES), lambda i, j: (i, COL_Q // LANES + j))] + _rope_specs(tm),
        out_specs=pl.BlockSpec((LANES, tm), lambda i, j: (j, i)),
        out_shape=jax.ShapeDtypeStruct((ATT_WIDTH, s), BF16),
        compiler_params=_params("parallel", "parallel"),
        name="q_prep",
    )(proj, cos, sin, g, bd, perm)


def _k_prep(proj, cos, sin, g, bd, perm, tm):
    s = proj.shape[0]
    return pl.pallas_call(
        _kprep_kernel,
        grid=(s // tm, ATT_KV_WIDTH // LANES),
        in_specs=[pl.BlockSpec((tm, LANES), lambda i, j: (i, COL_K // LANES + j))] + _rope_specs(tm),
        out_specs=pl.BlockSpec((2, tm, ATT_HEAD_DIM), lambda i, j: (j, i, 0)),
        out_shape=jax.ShapeDtypeStruct((ATT_KV_HEADS, s, ATT_HEAD_DIM), BF16),
        compiler_params=_params("parallel", "parallel"),
        name="k_prep",
    )(proj, cos, sin, g, bd, perm)


def _v_t(proj, tk):
    s = proj.shape[0]
    return pl.pallas_call(
        _vt_kernel,
        grid=(s // tk,),
        in_specs=[pl.BlockSpec((tk, ATT_KV_WIDTH), lambda i: (i, COL_V // ATT_KV_WIDTH))],
        out_specs=pl.BlockSpec((None, ATT_KV_WIDTH, tk), lambda i: (i, 0, 0)),
        out_shape=jax.ShapeDtypeStruct((s // tk, ATT_KV_WIDTH, tk), BF16),
        compiler_params=_params("parallel"),
        name="v_t",
    )(proj)


def _attn_kernel(qt_ref, k_ref, vt_ref, o_ref, *, tk):
    s = k_ref.shape[0]
    tq = qt_ref.shape[1]
    hd = ATT_HEAD_DIM
    outs = []
    for h in range(ATT_REP):
        qt = qt_ref[h * hd:(h + 1) * hd, :]

        def body(c, carry, qt=qt):
            m, l, acc = carry
            kc = k_ref[pl.ds(pl.multiple_of(c * tk, tk), tk), :]
            vtc = vt_ref[c]
            st = jnp.dot(kc, qt, preferred_element_type=F32)
            m_new = jnp.maximum(m, jnp.max(st, axis=0, keepdims=True))
            p = jnp.exp2(st - m_new)
            alpha = jnp.exp2(m - m_new)
            l = alpha * l + jnp.sum(p, axis=0, keepdims=True)
            acc = alpha * acc + jnp.dot(vtc, p.astype(BF16), preferred_element_type=F32)
            return m_new, l, acc

        init = (jnp.full((1, tq), -jnp.inf, F32), jnp.zeros((1, tq), F32), jnp.zeros((hd, tq), F32))
        m, l, acc = lax.fori_loop(0, s // tk, body, init)
        outs.append(acc / l)
    o_ref[...] = jnp.concatenate(outs, axis=0).T.astype(o_ref.dtype)


def _attention(qt, k, vt, tq):
    s = k.shape[1]
    tk = vt.shape[2]
    gw = ATT_REP * ATT_HEAD_DIM
    return pl.pallas_call(
        functools.partial(_attn_kernel, tk=tk),
        grid=(ATT_KV_HEADS, s // tq),
        in_specs=[
            pl.BlockSpec((gw, tq), lambda g, i: (g, i)),
            pl.BlockSpec((None, s, ATT_HEAD_DIM), lambda g, i: (g, 0, 0)),
            pl.BlockSpec((s // tk, ATT_HEAD_DIM, tk), lambda g, i: (0, g, 0)),
        ],
        out_specs=pl.BlockSpec((tq, gw), lambda g, i: (i, g)),
        out_shape=jax.ShapeDtypeStruct((s, ATT_WIDTH), BF16),
        compiler_params=_params("parallel", "parallel"),
        name="attention",
    )(qt, k, vt)


def _ssd_out_kernel(yf_ref, yb_ref, xs_ref, z_ref, dskip_ref, g_ref, w_ref, o_ref):
    y = yf_ref[...] + yb_ref[...] + xs_ref[...] * dskip_ref[...]
    y = y * _silu(z_ref[...])
    ms = jnp.mean(y * y, axis=-1, keepdims=True)
    yn = (y * lax.rsqrt(ms + EPS) * g_ref[...]).astype(BF16)
    o_ref[...] = jnp.dot(yn, w_ref[...], preferred_element_type=F32)


def _ssd_out(yf, yb, xbc, proj, dskip, g, w, tm):
    s = yf.shape[0]
    row = lambda i: (i, 0)
    const = lambda i: (0, 0)
    return pl.pallas_call(
        _ssd_out_kernel,
        grid=(s // tm,),
        in_specs=[
            pl.BlockSpec((tm, SSD_INNER), row), pl.BlockSpec((tm, SSD_INNER), row),
            pl.BlockSpec((tm, SSD_INNER), row), pl.BlockSpec((tm, SSD_INNER), row),
            pl.BlockSpec((1, SSD_INNER), const), pl.BlockSpec((1, SSD_INNER), const),
            pl.BlockSpec((SSD_INNER, D_MODEL), const),
        ],
        out_specs=pl.BlockSpec((tm, D_MODEL), row),
        out_shape=jax.ShapeDtypeStruct((s, D_MODEL), F32),
        compiler_params=_params("parallel"),
        name="ssd_out",
    )(yf, yb, xbc, proj, dskip, g, w)


def _pool_kernel(um_ref, up_ref, un_ref, wp_ref, scale_ref, w_ref, o_ref, ext_ref, *, seq):
    tm = um_ref.shape[0]
    _fill_ext(ext_ref, um_ref, up_ref, un_ref, tm)
    t = pl.program_id(0) * tm + lax.broadcasted_iota(I32, (tm, 1), 0)
    mixed = []
    for gi, win in enumerate(POOL_WINDOWS):
        half = win // 2
        cols = slice(gi * POOL_GROUP_DIM, (gi + 1) * POOL_GROUP_DIM)
        acc = ext_ref[pl.ds(HALO - half, tm), cols]
        for k in range(1, win):
            acc = acc + ext_ref[pl.ds(HALO - half + k, tm), cols]
        cnt = (jnp.minimum(t + half, seq) - jnp.maximum(t - half, 0)).astype(F32)
        mean = acc / cnt
        mix = (mean - um_ref[:, cols]).astype(BF16)
        mixed.append(jnp.dot(mix, wp_ref[gi], preferred_element_type=F32))
    pooled = (jnp.concatenate(mixed, axis=1) * scale_ref[...]).astype(BF16)
    o_ref[...] = jnp.dot(pooled, w_ref[...], preferred_element_type=F32)


def _pool(proj, w_pool, scale, w_br, tm):
    s = proj.shape[0]
    return pl.pallas_call(
        functools.partial(_pool_kernel, seq=s),
        grid=(s // tm,),
        in_specs=_halo_specs(tm, POOL_WIDTH, COL_POOL // POOL_WIDTH, s) + [
            pl.BlockSpec((len(POOL_WINDOWS), POOL_GROUP_DIM, POOL_GROUP_DIM), lambda i: (0, 0, 0)),
            pl.BlockSpec((1, POOL_WIDTH), lambda i: (0, 0)),
            pl.BlockSpec((POOL_WIDTH, D_MODEL), lambda i: (0, 0)),
        ],
        out_specs=pl.BlockSpec((tm, D_MODEL), lambda i: (i, 0)),
        out_shape=jax.ShapeDtypeStruct((s, D_MODEL), F32),
        scratch_shapes=[pltpu.VMEM((tm + 2 * HALO, POOL_WIDTH), F32)],
        compiler_params=_params("parallel"),
        name="pool",
    )(proj, proj, proj, w_pool, scale, w_br)


def _merge_kernel(bs_ref, att_ref, bp_ref, g0_ref, g1_ref, g2_ref, gb_ref, x_ref, watt_ref, wout_ref, o_ref):
    b_att = jnp.dot(att_ref[...], watt_ref[...], preferred_element_type=F32)
    gb = gb_ref[...]
    merged = (jax.nn.sigmoid(g0_ref[...] + gb[0:1]) * bs_ref[...]
              + jax.nn.sigmoid(g1_ref[...] + gb[1:2]) * b_att
              + jax.nn.sigmoid(g2_ref[...] + gb[2:3]) * bp_ref[...])
    o_ref[...] = x_ref[...] + jnp.dot(merged.astype(BF16), wout_ref[...], preferred_element_type=F32)


def _merge(b_ssd, att, b_pool, proj, gate_bias, x, w_att, w_out, tm):
    s = x.shape[0]
    row = lambda i: (i, 0)
    const = lambda i: (0, 0)
    gcol = COL_GATES // D_MODEL
    tile = pl.BlockSpec((tm, D_MODEL), row)
    return pl.pallas_call(
        _merge_kernel,
        grid=(s // tm,),
        in_specs=[
            tile, tile, tile,
            pl.BlockSpec((tm, D_MODEL), lambda i: (i, gcol)),
            pl.BlockSpec((tm, D_MODEL), lambda i: (i, gcol + 1)),
            pl.BlockSpec((tm, D_MODEL), lambda i: (i, gcol + 2)),
            pl.BlockSpec((N_BRANCH, D_MODEL), const),
            tile,
            pl.BlockSpec((ATT_WIDTH, D_MODEL), const),
            pl.BlockSpec((D_MODEL, D_MODEL), const),
        ],
        out_specs=tile,
        out_shape=jax.ShapeDtypeStruct((s, D_MODEL), F32),
        compiler_params=_params("parallel"),
        name="merge",
    )(b_ssd, att, b_pool, proj, proj, proj, gate_bias, x, w_att, w_out)


ROUTER_GROUP_ROW = 0
ROUTER_EXPERT_ROW = SUBLANES
TILE_TABLE_LANES = 256


def _first_argmax(v, idx, big):
    m = jnp.max(v, axis=0, keepdims=True)
    return m, jnp.min(jnp.where(v == m, idx, big), axis=0, keepdims=True)


def _router_kernel(x_ref, g_ref, wt_ref, b_ref, upper_ref, ltri_ref,
                   h_ref, eidx_ref, rank_ref, wts_ref, off_ref, tiles_ref, count_ref, *, tile_rows):
    i = pl.program_id(0)
    tm = x_ref.shape[0]

    @pl.when(i == 0)
    def _():
        count_ref[...] = jnp.zeros_like(count_ref)

    x = x_ref[...]
    ms = jnp.mean(x * x, axis=-1, keepdims=True)
    h = x * lax.rsqrt(ms + EPS) * g_ref[...]
    h_ref[...] = h
    lt = lax.dot_general(wt_ref[...], h, (((1,), (1,)), ((), ())), precision=HIGHEST,
                         preferred_element_type=F32) + b_ref[...]

    r8 = lax.broadcasted_iota(I32, (SUBLANES, tm), 0)
    gl = jnp.where(r8 < MOE_GROUPS, lt[ROUTER_GROUP_ROW:ROUTER_GROUP_ROW + SUBLANES], -jnp.inf)
    gmax, gsel = _first_argmax(gl, r8, SUBLANES)
    p_group = 1.0 / jnp.sum(jnp.exp(gl - gmax), axis=0, keepdims=True)

    r32 = lax.broadcasted_iota(I32, (N_EXPERTS, tm), 0)
    el = lt[ROUTER_EXPERT_ROW:ROUTER_EXPERT_ROW + N_EXPERTS]
    el = jnp.where(lax.shift_right_logical(r32, int(math.log2(MOE_PER_GROUP))) == gsel, el, -jnp.inf)
    m1, i1 = _first_argmax(el, r32, N_EXPERTS)
    el2 = jnp.where(r32 == i1, -jnp.inf, el)
    m2, i2 = _first_argmax(el2, r32, N_EXPERTS)
    z = jnp.sum(jnp.exp(el - m1), axis=0, keepdims=True)
    p1 = 1.0 / z
    p2 = jnp.exp(m2 - m1) / z
    psum = p1 + p2
    wts_ref[0:1, :] = p_group * (p1 / psum)
    wts_ref[1:2, :] = p_group * (p2 / psum)
    eidx_ref[0:1, :] = i1
    eidx_ref[1:2, :] = i2

    sel0 = r32 == i1
    sel1 = r32 == i2
    member = jnp.where(sel0 | sel1, 1.0, 0.0)
    before = jnp.dot(member.astype(BF16), upper_ref[...], preferred_element_type=F32) + count_ref[:, 0:1]
    rank_ref[0:1, :] = jnp.sum(jnp.where(sel0, before, 0.0), axis=0, keepdims=True).astype(I32)
    rank_ref[1:2, :] = jnp.sum(jnp.where(sel1, before, 0.0), axis=0, keepdims=True).astype(I32)
    count_ref[...] = count_ref[...] + jnp.sum(member, axis=1, keepdims=True)

    @pl.when(i == pl.num_programs(0) - 1)
    def _():
        cnt = count_ref[...]
        ntile = jnp.floor((cnt + (tile_rows - 1)) * (1.0 / tile_rows))
        start = jnp.dot(ltri_ref[...], ntile.astype(BF16), preferred_element_type=F32)
        off_ref[...] = (start * tile_rows).astype(I32)
        end = (start + ntile)[:, 0:1]
        tid = lax.broadcasted_iota(I32, (N_EXPERTS, TILE_TABLE_LANES), 1).astype(F32)
        owner = jnp.sum(jnp.where(end <= tid, 1.0, 0.0), axis=0, keepdims=True)
        owner = jnp.minimum(owner, N_EXPERTS - 1.0)
        nact = jnp.sum(ntile[:, 0:1], axis=0, keepdims=True)
        r = lax.broadcasted_iota(I32, (SUBLANES, TILE_TABLE_LANES), 0)
        tiles_ref[...] = jnp.where(r == 0, owner, nact).astype(I32)


def _router(x, g, wt, b, upper, ltri, tm, tile_rows):
    s = x.shape[0]
    row = lambda i: (i, 0)
    const = lambda i: (0, 0)
    lane = lambda i: (0, i)
    return pl.pallas_call(
        functools.partial(_router_kernel, tile_rows=tile_rows),
        grid=(s // tm,),
        in_specs=[
            pl.BlockSpec((tm, D_MODEL), row),
            pl.BlockSpec((1, D_MODEL), const),
            pl.BlockSpec((LANES, D_MODEL), const),
            pl.BlockSpec((LANES, 1), const),
            pl.BlockSpec((tm, tm), const),
            pl.BlockSpec((N_EXPERTS, N_EXPERTS), const),
        ],
        out_specs=[
            pl.BlockSpec((tm, D_MODEL), row),
            pl.BlockSpec((2, tm), lane),
            pl.BlockSpec((2, tm), lane),
            pl.BlockSpec((2, tm), lane),
            pl.BlockSpec((N_EXPERTS, LANES), const),
            pl.BlockSpec((SUBLANES, TILE_TABLE_LANES), const),
        ],
        out_shape=[
            jax.ShapeDtypeStruct((s, D_MODEL), F32),
            jax.ShapeDtypeStruct((2, s), I32),
            jax.ShapeDtypeStruct((2, s), I32),
            jax.ShapeDtypeStruct((2, s), F32),
            jax.ShapeDtypeStruct((N_EXPERTS, LANES), I32),
            jax.ShapeDtypeStruct((SUBLANES, TILE_TABLE_LANES), I32),
        ],
        scratch_shapes=[pltpu.VMEM((N_EXPERTS, LANES), F32)],
        compiler_params=_params("arbitrary"),
        name="router",
    )(x, g, wt, b, upper, ltri)


def _sorted_row(eidx_s, rank_s, off_s, slot, tok, seq):
    return off_s[eidx_s[slot * seq + tok]] + rank_s[slot * seq + tok]


def _dispatch_kernel(eidx_s, rank_s, off_s, h_hbm, xs_in_hbm, xs_hbm, sem, *, tile, seq):
    del xs_in_hbm
    base = pl.program_id(0) * tile

    def row_copy(tok, slot):
        pos = _sorted_row(eidx_s, rank_s, off_s, slot, tok, seq)
        return pltpu.make_async_copy(h_hbm.at[pl.ds(tok, 1), :], xs_hbm.at[pl.ds(pos, 1), :], sem)

    def start(t, c):
        row_copy(base + t, 0).start()
        row_copy(base + t, 1).start()
        return c

    def wait(t, c):
        row_copy(base + t, 0).wait()
        row_copy(base + t, 1).wait()
        return c

    lax.fori_loop(0, tile, start, 0)
    lax.fori_loop(0, tile, wait, 0)


def _dispatch(eidx, rank, off, h, xs_zero, tile):
    s = h.shape[0]
    return pl.pallas_call(
        functools.partial(_dispatch_kernel, tile=tile, seq=s),
        grid_spec=pltpu.PrefetchScalarGridSpec(
            num_scalar_prefetch=3,
            grid=(s // tile,),
            in_specs=[pl.BlockSpec(memory_space=pl.ANY), pl.BlockSpec(memory_space=pl.ANY)],
            out_specs=pl.BlockSpec(memory_space=pl.ANY),
            scratch_shapes=[pltpu.SemaphoreType.DMA],
        ),
        out_shape=jax.ShapeDtypeStruct(xs_zero.shape, xs_zero.dtype),
        input_output_aliases={4: 0},
        compiler_params=_params("arbitrary"),
        name="dispatch",
    )(eidx, rank, off, h, xs_zero)


def _ffn_kernel(owner_s, nact_s, x_ref, wg_ref, wu_ref, wd_ref, o_ref):
    j = pl.program_id(0)

    @pl.when(j < nact_s[0])
    def _():
        x = x_ref[...].astype(BF16)
        gate = jnp.dot(x, wg_ref[...], preferred_element_type=F32)
        up = jnp.dot(x, wu_ref[...], preferred_element_type=F32)
        hid = (_silu(gate) * up).astype(BF16)
        o_ref[...] = jnp.dot(hid, wd_ref[...], preferred_element_type=F32)

    @pl.when(j >= nact_s[0])
    def _():
        o_ref[...] = jnp.zeros_like(o_ref)


def _expert_ffn(owner, nact, xs, wg, wu, wd, tile_rows):
    rows = xs.shape[0]
    return pl.pallas_call(
        _ffn_kernel,
        grid_spec=pltpu.PrefetchScalarGridSpec(
            num_scalar_prefetch=2,
            grid=(rows // tile_rows,),
            in_specs=[
                pl.BlockSpec((tile_rows, D_MODEL), lambda j, o, n: (j, 0)),
                pl.BlockSpec((None, D_MODEL, EXPERT_FF), lambda j, o, n: (o[j], 0, 0)),
                pl.BlockSpec((None, D_MODEL, EXPERT_FF), lambda j, o, n: (o[j], 0, 0)),
                pl.BlockSpec((None, EXPERT_FF, D_MODEL), lambda j, o, n: (o[j], 0, 0)),
            ],
            out_specs=pl.BlockSpec((tile_rows, D_MODEL), lambda j, o, n: (j, 0)),
        ),
        out_shape=jax.ShapeDtypeStruct((rows, D_MODEL), F32),
        compiler_params=_params("arbitrary"),
        name="expert_ffn",
    )(owner, nact, xs, wg, wu, wd)


def _combine_kernel(eidx_s, rank_s, off_s, ys_hbm, x_ref, w_ref, o_ref, buf_ref, sem, *, seq):
    tile = x_ref.shape[0]
    base = pl.program_id(0) * tile

    def row_copy(t, slot):
        pos = _sorted_row(eidx_s, rank_s, off_s, slot, base + t, seq)
        return pltpu.make_async_copy(ys_hbm.at[pl.ds(pos, 1), :], buf_ref.at[slot, pl.ds(t, 1), :], sem)

    def start(t, c):
        row_copy(t, 0).start()
        row_copy(t, 1).start()
        return c

    def wait(t, c):
        row_copy(t, 0).wait()
        row_copy(t, 1).wait()
        return c

    lax.fori_loop(0, tile, start, 0)
    lax.fori_loop(0, tile, wait, 0)
    w = w_ref[...]
    o_ref[...] = x_ref[...] + w[:, 0:1] * buf_ref[0] + w[:, 1:2] * buf_ref[1]


def _combine(eidx, rank, off, ys, x, w_cols, tile):
    s = x.shape[0]
    return pl.pallas_call(
        functools.partial(_combine_kernel, seq=s),
        grid_spec=pltpu.PrefetchScalarGridSpec(
            num_scalar_prefetch=3,
            grid=(s // tile,),
            in_specs=[
                pl.BlockSpec(memory_space=pl.ANY),
                pl.BlockSpec((tile, D_MODEL), lambda i, *_: (i, 0)),
                pl.BlockSpec((tile, 2), lambda i, *_: (i, 0)),
            ],
            out_specs=pl.BlockSpec((tile, D_MODEL), lambda i, *_: (i, 0)),
            scratch_shapes=[pltpu.VMEM((2, tile, D_MODEL), F32), pltpu.SemaphoreType.DMA],
        ),
        out_shape=jax.ShapeDtypeStruct((s, D_MODEL), F32),
        compiler_params=_params("arbitrary"),
        name="combine",
    )(eidx, rank, off, ys, x, w_cols)


def _tiles(s):
    return dict(
        inproj_tm=min(1024, s), inproj_tn=768,
        conv_tm=min(512, s), prep_tm=min(1024, s),
        attn_tq=256, attn_tk=min(512, s),
        tail_tm=min(256, s), pool_tm=min(512, s),
        router_tm=min(512, s), moe_rows=256, dispatch_tile=min(512, s), combine_tile=min(256, s),
    )


def _rope_tables(s):
    rows = s // GRID_W
    row = jnp.repeat(jnp.arange(rows), GRID_W).astype(F32)
    col = jnp.tile(jnp.arange(GRID_W), rows).astype(F32)
    half = ATT_HEAD_DIM // 2
    inv_freq = ROPE_THETA ** (-jnp.arange(0, half, 2, dtype=F32) / half)
    ang = jnp.concatenate([row[:, None] * inv_freq, col[:, None] * inv_freq], axis=-1)
    d = jnp.arange(LANES)
    idx = ((d % ATT_HEAD_DIM) // half) * (half // 2) + d % (half // 2)
    return jnp.cos(ang)[:, idx], jnp.sin(ang)[:, idx]


def _head_constants():
    d = jnp.arange(LANES)
    same_head = (d[:, None] // ATT_HEAD_DIM) == (d[None, :] // ATT_HEAD_DIM)
    bd = same_head.astype(BF16)
    quarter = ATT_HEAD_DIM // 4
    second = (d % (2 * quarter)) >= quarter
    src = jnp.where(second, d - quarter, d + quarter)
    sign = jnp.where(second, 1.0, -1.0)
    perm = jnp.zeros((LANES, LANES), F32).at[src, d].set(sign).astype(BF16)
    return bd, perm


def _expand_matrix(lane0):
    r = jnp.arange(LANES)[:, None]
    c = jnp.arange(SSD_INNER)[None, :]
    return (r == lane0 + c // SSD_HEAD_DIM).astype(BF16)


def _pad_lanes(v, width=LANES):
    flat = v.reshape(1, -1)
    return jnp.pad(flat, ((0, 0), (0, width - flat.shape[1])))


def _mixer_layer(x, t, cos, sin, bd, perm, e_f, e_b, norm_mix, w_in, conv_w, conv_b, dt_bias, a_log, d_skip,
                 ssd_norm, w_br_ssd, q_norm, k_norm, w_br_att, w_pool, pool_scale, w_br_pool, gate_bias, w_out):
    o_z, o_xbc = 0, SSD_INNER
    o_dt = o_xbc + SSD_CONV_DIM
    o_q = o_dt + 2 * SSD_HEADS
    o_kv = o_q + ATT_WIDTH
    o_pool = o_kv + 2 * ATT_KV_WIDTH
    o_gate = o_pool + POOL_WIDTH
    w_main = jnp.concatenate(
        [w_in[:, o_z:o_dt], w_in[:, o_gate:], w_in[:, o_q:o_kv], w_in[:, o_pool:o_gate], w_in[:, o_kv:o_pool]],
        axis=1).astype(BF16)
    w_dt = jnp.pad(w_in[:, o_dt:o_q], ((0, 0), (0, DT_PAD - 2 * SSD_HEADS))).astype(BF16)

    proj, dt_raw = _in_proj(x, norm_mix.reshape(1, -1), w_main, w_dt, t["inproj_tm"], t["inproj_tn"])

    xbc = _conv(proj, conv_w, conv_b.reshape(1, -1), t["conv_tm"])
    y_f, y_b = _ssd(xbc, dt_raw, _pad_lanes(dt_bias), _pad_lanes(a_log), e_f, e_b)
    b_ssd = _ssd_out(y_f, y_b, xbc, proj, jnp.repeat(d_skip, SSD_HEAD_DIM).reshape(1, -1),
                     ssd_norm.reshape(1, -1), w_br_ssd.astype(BF16), t["tail_tm"])

    qn = jnp.tile(q_norm, LANES // ATT_HEAD_DIM).reshape(1, -1)
    kn = jnp.tile(k_norm, LANES // ATT_HEAD_DIM).reshape(1, -1)
    qt = _q_prep(proj, cos, sin, qn, bd, perm, t["prep_tm"])
    k = _k_prep(proj, cos, sin, kn, bd, perm, t["prep_tm"])
    vt = _v_t(proj, t["attn_tk"])
    att = _attention(qt, k, vt, t["attn_tq"])

    b_pool = _pool(proj, w_pool.astype(BF16), pool_scale.reshape(1, -1), w_br_pool.astype(BF16), t["pool_tm"])
    return _merge(b_ssd, att, b_pool, proj, gate_bias, x, w_br_att.astype(BF16), w_out.astype(BF16), t["tail_tm"])


def _moe_layer(x, t, upper, ltri, norm_ffn, w_rg, b_rg, w_re, b_re, w_gate_e, w_up_e, w_down_e):
    s = x.shape[0]
    rows_per_tile = t["moe_rows"]
    wt = jnp.zeros((LANES, D_MODEL), F32)
    wt = wt.at[ROUTER_GROUP_ROW:ROUTER_GROUP_ROW + MOE_GROUPS].set(w_rg.T)
    wt = wt.at[ROUTER_EXPERT_ROW:ROUTER_EXPERT_ROW + N_EXPERTS].set(w_re.T)
    bias = jnp.zeros((LANES, 1), F32)
    bias = bias.at[ROUTER_GROUP_ROW:ROUTER_GROUP_ROW + MOE_GROUPS, 0].set(b_rg)
    bias = bias.at[ROUTER_EXPERT_ROW:ROUTER_EXPERT_ROW + N_EXPERTS, 0].set(b_re)

    h, eidx, rank, wts, off, tiles = _router(x, norm_ffn.reshape(1, -1), wt, bias, upper, ltri,
                                             t["router_tm"], rows_per_tile)
    eidx, rank, off = eidx.reshape(-1), rank.reshape(-1), off[:, 0]
    sorted_rows = 2 * s + N_EXPERTS * rows_per_tile
    xs = _dispatch(eidx, rank, off, h, jnp.zeros((sorted_rows, D_MODEL), F32), t["dispatch_tile"])
    ys = _expert_ffn(tiles[0], tiles[1, 0:1], xs, w_gate_e.astype(BF16), w_up_e.astype(BF16),
                     w_down_e.astype(BF16), rows_per_tile)
    return _combine(eidx, rank, off, ys, x, wts.T, t["combine_tile"])


def kernel(x, norm_mix, w_in, conv_w, conv_b, dt_bias, a_log, d_skip, ssd_norm, w_br_ssd, q_norm, k_norm, w_br_att, w_pool, pool_scale, w_br_pool, gate_bias, w_out, norm_ffn, w_router_group, b_router_group, w_router_expert, b_router_expert, w_gate_e, w_up_e, w_down_e):
    b, s, _ = x.shape
    assert b == 1 and s % 1024 == 0 and s % GRID_W == 0
    t = _tiles(s)
    assert 2 * s // t["moe_rows"] + N_EXPERTS <= TILE_TABLE_LANES
    cos, sin = _rope_tables(s)
    bd, perm = _head_constants()
    e_f, e_b = _expand_matrix(0), _expand_matrix(SSD_HEADS)
    rt = t["router_tm"]
    upper = (jnp.arange(rt)[:, None] < jnp.arange(rt)[None, :]).astype(BF16)
    ltri = (jnp.arange(N_EXPERTS)[None, :] < jnp.arange(N_EXPERTS)[:, None]).astype(BF16)

    xx = x[0]
    for l in range(DEPTH):
        xx = _mixer_layer(xx, t, cos, sin, bd, perm, e_f, e_b, norm_mix[l], w_in[l], conv_w[l], conv_b[l], dt_bias[l],
                          a_log[l], d_skip[l], ssd_norm[l], w_br_ssd[l], q_norm[l], k_norm[l], w_br_att[l],
                          w_pool[l], pool_scale[l], w_br_pool[l], gate_bias[l], w_out[l])
        xx = _moe_layer(xx, t, upper, ltri, norm_ffn[l], w_router_group[l], b_router_group[l],
                        w_router_expert[l], b_router_expert[l], w_gate_e[l], w_up_e[l], w_down_e[l])
    return xx[None]
```

```python
import functools
import math

import jax
import jax.numpy as jnp
from jax import lax
from jax.experimental import pallas as pl
from jax.experimental.pallas import tpu as pltpu

F32 = jnp.float32
BF16 = jnp.bfloat16
I32 = jnp.int32
HIGHEST = lax.Precision.HIGHEST

EPS = 1e-6
D_MODEL = 1024
DEPTH = 2

SSD_INNER = 2048
SSD_HEAD_DIM = 64
SSD_HEADS = 32
SSD_GROUPS = 4
SSD_STATE = 128
SSD_CONV = 5
SSD_CHUNK = 128
SSD_BC = SSD_GROUPS * SSD_STATE
SSD_CONV_DIM = SSD_INNER + 2 * SSD_BC
HEADS_PER_GROUP = SSD_HEADS // SSD_GROUPS
GROUP_WIDTH = HEADS_PER_GROUP * SSD_HEAD_DIM

ATT_HEADS = 16
ATT_KV_HEADS = 4
ATT_HEAD_DIM = 64
ATT_WIDTH = 1024
ATT_KV_WIDTH = 256
ATT_REP = ATT_HEADS // ATT_KV_HEADS
ROPE_THETA = 10000.0
GRID_W = 64

POOL_WIDTH = 1024
POOL_WINDOWS = (2, 4, 8, 16)
POOL_GROUP_DIM = 256

N_BRANCH = 3
MOE_GROUPS = 4
MOE_PER_GROUP = 8
N_EXPERTS = 32
EXPERT_FF = 512

LANES = 128
SUBLANES = 8
VMEM_LIMIT_BYTES = 56 * 1024 * 1024

COL_Z = 0
COL_XBC = COL_Z + SSD_INNER
COL_GATES = COL_XBC + SSD_CONV_DIM
COL_Q = COL_GATES + N_BRANCH * D_MODEL
COL_POOL = COL_Q + ATT_WIDTH
COL_K = COL_POOL + POOL_WIDTH
COL_V = COL_K + ATT_KV_WIDTH
PROJ_DIM = COL_V + ATT_KV_WIDTH
DT_PAD = LANES

HALO = SUBLANES


def _params(*sem):
    return pltpu.CompilerParams(dimension_semantics=sem, vmem_limit_bytes=VMEM_LIMIT_BYTES)


def _silu(v):
    return v * jax.nn.sigmoid(v)


def _split_bf16(v):
    hi = v.astype(BF16)
    lo = (v - hi.astype(F32)).astype(BF16)
    return hi, lo


def _inproj_kernel(x_ref, g_ref, w_ref, wdt_ref, o_ref, dt_ref, h_ref):
    @pl.when(pl.program_id(1) == 0)
    def _():
        x = x_ref[...]
        ms = jnp.mean(x * x, axis=-1, keepdims=True)
        h = (x * lax.rsqrt(ms + EPS) * g_ref[...]).astype(BF16)
        h_ref[...] = h
        dt_ref[...] = jnp.dot(h, wdt_ref[...], preferred_element_type=F32)

    o_ref[...] = jnp.dot(h_ref[...], w_ref[...], preferred_element_type=F32)


def _in_proj(x, g, w, wdt, tm, tn):
    s = x.shape[0]
    return pl.pallas_call(
        _inproj_kernel,
        grid=(s // tm, PROJ_DIM // tn),
        in_specs=[
            pl.BlockSpec((tm, D_MODEL), lambda i, j: (i, 0)),
            pl.BlockSpec((1, D_MODEL), lambda i, j: (0, 0)),
            pl.BlockSpec((D_MODEL, tn), lambda i, j: (0, j)),
            pl.BlockSpec((D_MODEL, DT_PAD), lambda i, j: (0, 0)),
        ],
        out_specs=[
            pl.BlockSpec((tm, tn), lambda i, j: (i, j)),
            pl.BlockSpec((tm, DT_PAD), lambda i, j: (i, 0)),
        ],
        out_shape=[jax.ShapeDtypeStruct((s, PROJ_DIM), F32), jax.ShapeDtypeStruct((s, DT_PAD), F32)],
        scratch_shapes=[pltpu.VMEM((tm, D_MODEL), BF16)],
        compiler_params=_params("parallel", "arbitrary"),
        name="in_proj",
    )(x, g, w, wdt)


def _fill_ext(ext_ref, main_ref, prev_ref, next_ref, tm):
    i = pl.program_id(0)
    last = pl.num_programs(0) - 1
    ext_ref[0:HALO, :] = jnp.where(i > 0, prev_ref[...], 0.0)
    ext_ref[HALO:HALO + tm, :] = main_ref[...]
    ext_ref[HALO + tm:, :] = jnp.where(i < last, next_ref[...], 0.0)


def _conv_kernel(xm_ref, xp_ref, xn_ref, w_ref, b_ref, o_ref, ext_ref):
    tm = xm_ref.shape[0]
    _fill_ext(ext_ref, xm_ref, xp_ref, xn_ref, tm)
    pad = SSD_CONV // 2
    acc = jnp.broadcast_to(b_ref[...], o_ref.shape)
    for k in range(SSD_CONV):
        acc = acc + ext_ref[pl.ds(HALO - pad + k, tm), :] * w_ref[k:k + 1, :]
    o_ref[...] = _silu(acc)


def _halo_specs(tm, tc, col0, s):
    rb = tm // HALO
    nrb = s // HALO
    return [
        pl.BlockSpec((tm, tc), lambda i, j=0: (i, col0 + j)),
        pl.BlockSpec((HALO, tc), lambda i, j=0: (jnp.maximum(i * rb - 1, 0), col0 + j)),
        pl.BlockSpec((HALO, tc), lambda i, j=0: (jnp.minimum((i + 1) * rb, nrb - 1), col0 + j)),
    ]


def _conv(proj, w, b, tm):
    s = proj.shape[0]
    tc = 1024
    return pl.pallas_call(
        _conv_kernel,
        grid=(s // tm, SSD_CONV_DIM // tc),
        in_specs=_halo_specs(tm, tc, COL_XBC // tc, s) + [
            pl.BlockSpec((SSD_CONV, tc), lambda i, j: (0, j)),
            pl.BlockSpec((1, tc), lambda i, j: (0, j)),
        ],
        out_specs=pl.BlockSpec((tm, tc), lambda i, j: (i, j)),
        out_shape=jax.ShapeDtypeStruct((s, SSD_CONV_DIM), F32),
        scratch_shapes=[pltpu.VMEM((tm + 2 * HALO, tc), F32)],
        compiler_params=_params("parallel", "parallel"),
        name="conv",
    )(proj, proj, proj, w, b)


def _ssd_direction(x_ref, b_ref, c_ref, dt_ref, e_ref, y_ref, st_ref, bias, a, fwd):
    q = SSD_CHUNK
    row = lax.broadcasted_iota(I32, (q, q), 0)
    col = lax.broadcasted_iota(I32, (q, q), 1)
    mask = (col <= row) if fwd else (col >= row)
    lane0 = 0 if fwd else SSD_HEADS

    dt = jax.nn.softplus(dt_ref[...] + bias)
    da = dt * a
    cs = jnp.dot(mask.astype(F32), da, precision=HIGHEST, preferred_element_type=F32)
    cs_t = cs.T
    edge = q - 1 if fwd else 0
    tot = cs[edge:edge + 1, :]
    dt_hi, dt_lo = _split_bf16(dt)
    dec = jnp.exp(tot)
    dec_hi = dec.astype(BF16).astype(F32)
    r16 = lax.broadcasted_iota(I32, (2 * SUBLANES, LANES), 0)
    dec_rows = jnp.where(r16 == 0, dec_hi, jnp.where(r16 == 1, dec - dec_hi, 0.0)).astype(BF16)
    lhs = jnp.concatenate(
        [dt_hi, dt_lo, jnp.exp(cs).astype(BF16), jnp.exp(tot - cs).astype(BF16), dec_rows], axis=0)
    ex = jnp.dot(lhs, e_ref[...], preferred_element_type=F32)
    dt_x = ex[0:q] + ex[q:2 * q]
    ecs_x = ex[2 * q:3 * q]
    dte_x = ex[3 * q:4 * q]
    dec_x = ex[4 * q:4 * q + 1] + ex[4 * q + 1:4 * q + 2]

    xdt = x_ref[...] * dt_x
    xdte_b = (xdt * dte_x).astype(BF16)
    lane = lax.broadcasted_iota(I32, (q, LANES), 1)
    first_half = lane < SSD_HEAD_DIM
    bm = b_ref[...].astype(BF16)
    cm = c_ref[...].astype(BF16)
    for g in range(SSD_GROUPS):
        bg = bm[:, g * SSD_STATE:(g + 1) * SSD_STATE]
        cg = cm[:, g * SSD_STATE:(g + 1) * SSD_STATE]
        cb = lax.dot_general(cg, bg, (((1,), (1,)), ((), ())), preferred_element_type=F32)
        gs = slice(g * GROUP_WIDTH, (g + 1) * GROUP_WIDTH)
        st = st_ref[g]
        y_off = jnp.dot(cg, st.astype(BF16), preferred_element_type=F32) * ecs_x[:, gs]
        for pr in range(HEADS_PER_GROUP // 2):
            c0 = g * GROUP_WIDTH + pr * LANES
            xp = xdt[:, c0:c0 + LANES]
            halves = (jnp.where(first_half, xp, 0.0).astype(BF16), jnp.where(first_half, 0.0, xp).astype(BF16))
            yd = None
            for k in range(2):
                hl = lane0 + g * HEADS_PER_GROUP + 2 * pr + k
                seg = cs[:, hl:hl + 1] - cs_t[hl:hl + 1, :]
                decay = jnp.exp(jnp.where(mask, seg, -jnp.inf))
                part = jnp.dot((cb * decay).astype(BF16), halves[k], preferred_element_type=F32)
                yd = part if yd is None else yd + part
            y_ref[:, c0:c0 + LANES] = yd + y_off[:, pr * LANES:(pr + 1) * LANES]
        upd = lax.dot_general(bg, xdte_b[:, gs], (((0,), (0,)), ((), ())), preferred_element_type=F32)
        st_ref[g] = st * dec_x[:, gs] + upd


def _ssd_kernel(xf_ref, bf_ref, cf_ref, dtf_ref, xb_ref, bb_ref, cb_ref, dtb_ref,
                bias_ref, alog_ref, ef_ref, eb_ref, yf_ref, yb_ref, stf_ref, stb_ref):
    @pl.when(pl.program_id(0) == 0)
    def _():
        stf_ref[...] = jnp.zeros_like(stf_ref)
        stb_ref[...] = jnp.zeros_like(stb_ref)

    bias = bias_ref[...]
    a = -jnp.exp(alog_ref[...])
    _ssd_direction(xf_ref, bf_ref, cf_ref, dtf_ref, ef_ref, yf_ref, stf_ref, bias, a, True)
    _ssd_direction(xb_ref, bb_ref, cb_ref, dtb_ref, eb_ref, yb_ref, stb_ref, bias, a, False)


def _ssd(xbc, dt_raw, dt_bias, a_log, e_f, e_b):
    s = xbc.shape[0]
    q = SSD_CHUNK
    nc = s // q
    nb = SSD_INNER // SSD_BC

    def chunk_specs(im):
        return [
            pl.BlockSpec((q, SSD_INNER), lambda i: (im(i), 0)),
            pl.BlockSpec((q, SSD_BC), lambda i: (im(i), nb)),
            pl.BlockSpec((q, SSD_BC), lambda i: (im(i), nb + 1)),
            pl.BlockSpec((q, DT_PAD), lambda i: (im(i), 0)),
        ]

    fwd = lambda i: i
    bwd = lambda i: nc - 1 - i
    const = lambda shape: pl.BlockSpec(shape, lambda i: (0,) * len(shape))
    st_shape = (SSD_GROUPS, SSD_STATE, GROUP_WIDTH)
    return pl.pallas_call(
        _ssd_kernel,
        grid=(nc,),
        in_specs=chunk_specs(fwd) + chunk_specs(bwd) + [
            const((1, DT_PAD)), const((1, DT_PAD)), const((LANES, SSD_INNER)), const((LANES, SSD_INNER))],
        out_specs=[pl.BlockSpec((q, SSD_INNER), lambda i: (fwd(i), 0)),
                   pl.BlockSpec((q, SSD_INNER), lambda i: (bwd(i), 0))],
        out_shape=[jax.ShapeDtypeStruct((s, SSD_INNER), F32)] * 2,
        scratch_shapes=[pltpu.VMEM(st_shape, F32), pltpu.VMEM(st_shape, F32)],
        compiler_params=_params("arbitrary"),
        name="ssd",
    )(xbc, xbc, xbc, dt_raw, xbc, xbc, xbc, dt_raw, dt_bias, a_log, e_f, e_b)


def _norm_rope(x, cos, sin, g, bd, perm):
    hi, lo = _split_bf16(x * x)
    ss = jnp.dot(hi, bd, preferred_element_type=F32) + jnp.dot(lo, bd, preferred_element_type=F32)
    y = x * lax.rsqrt(ss * (1.0 / ATT_HEAD_DIM) + EPS) * g
    yh, yl = _split_bf16(y)
    rot = jnp.dot(yh, perm, preferred_element_type=F32) + jnp.dot(yl, perm, preferred_element_type=F32)
    return y * cos + rot * sin


def _qprep_kernel(x_ref, cos_ref, sin_ref, g_ref, bd_ref, perm_ref, o_ref, *, scale):
    tq = x_ref.shape[0]
    cos, sin, g, bd, perm = cos_ref[...], sin_ref[...], g_ref[...], bd_ref[...], perm_ref[...]
    for j in range(ATT_WIDTH // LANES):
        out = _norm_rope(x_ref[:, j * LANES:(j + 1) * LANES], cos, sin, g, bd, perm) * scale
        out_t = out.T.astype(BF16)
        grp, r0 = (2 * j) // ATT_REP, (2 * j) % ATT_REP
        o_ref[grp, :, r0 * tq:(r0 + 1) * tq] = out_t[:ATT_HEAD_DIM]
        o_ref[grp, :, (r0 + 1) * tq:(r0 + 2) * tq] = out_t[ATT_HEAD_DIM:]


def _kprep_kernel(x_ref, cos_ref, sin_ref, g_ref, bd_ref, perm_ref, o_ref):
    out = _norm_rope(x_ref[...], cos_ref[...], sin_ref[...], g_ref[...], bd_ref[...], perm_ref[...]).astype(BF16)
    o_ref[0] = out[:, :ATT_HEAD_DIM]
    o_ref[1] = out[:, ATT_HEAD_DIM:]


def _vt_kernel(x_ref, o_ref):
    o_ref[...] = x_ref[...].T.astype(BF16)


def _rope_specs(tm):
    return [
        pl.BlockSpec((tm, LANES), lambda i, j: (i, 0)),
        pl.BlockSpec((tm, LANES), lambda i, j: (i, 0)),
        pl.BlockSpec((1, LANES), lambda i, j: (0, 0)),
        pl.BlockSpec((LANES, LANES), lambda i, j: (0, 0)),
        pl.BlockSpec((LANES, LANES), lambda i, j: (0, 0)),
    ]


def _q_prep(proj, cos, sin, g, bd, perm, tq):
    s = proj.shape[0]
    scale = ATT_HEAD_DIM ** -0.5 * math.log2(math.e)
    const = lambda i: (0, 0)
    return pl.pallas_call(
        functools.partial(_qprep_kernel, scale=scale),
        grid=(s // tq,),
        in_specs=[
            pl.BlockSpec((tq, ATT_WIDTH), lambda i: (i, COL_Q // ATT_WIDTH)),
            pl.BlockSpec((tq, LANES), lambda i: (i, 0)),
            pl.BlockSpec((tq, LANES), lambda i: (i, 0)),
            pl.BlockSpec((1, LANES), const),
            pl.BlockSpec((LANES, LANES), const),
            pl.BlockSpec((LANES, LANES), const),
        ],
        out_specs=pl.BlockSpec((ATT_KV_HEADS, ATT_HEAD_DIM, ATT_REP * tq), lambda i: (0, 0, i)),
        out_shape=jax.ShapeDtypeStruct((ATT_KV_HEADS, ATT_HEAD_DIM, ATT_REP * s), BF16),
        compiler_params=_params("parallel"),
        name="q_prep",
    )(proj, cos, sin, g, bd, perm)


def _k_prep(proj, cos, sin, g, bd, perm, tm):
    s = proj.shape[0]
    return pl.pallas_call(
        _kprep_kernel,
        grid=(s // tm, ATT_KV_WIDTH // LANES),
        in_specs=[pl.BlockSpec((tm, LANES), lambda i, j: (i, COL_K // LANES + j))] + _rope_specs(tm),
        out_specs=pl.BlockSpec((2, tm, ATT_HEAD_DIM), lambda i, j: (j, i, 0)),
        out_shape=jax.ShapeDtypeStruct((ATT_KV_HEADS, s, ATT_HEAD_DIM), BF16),
        compiler_params=_params("parallel", "parallel"),
        name="k_prep",
    )(proj, cos, sin, g, bd, perm)


def _v_t(proj, tk):
    s = proj.shape[0]
    return pl.pallas_call(
        _vt_kernel,
        grid=(s // tk,),
        in_specs=[pl.BlockSpec((tk, ATT_KV_WIDTH), lambda i: (i, COL_V // ATT_KV_WIDTH))],
        out_specs=pl.BlockSpec((None, ATT_KV_WIDTH, tk), lambda i: (i, 0, 0)),
        out_shape=jax.ShapeDtypeStruct((s // tk, ATT_KV_WIDTH, tk), BF16),
        compiler_params=_params("parallel"),
        name="v_t",
    )(proj)


def _attn_kernel(qt_ref, k_ref, vt_ref, o_ref, m_ref, l_ref, acc_ref, sa_ref, sb_ref, *, tk):
    s = k_ref.shape[0]
    nq = qt_ref.shape[1]
    m_ref[...] = jnp.full_like(m_ref, -jnp.inf)
    l_ref[...] = jnp.zeros_like(l_ref)
    acc_ref[...] = jnp.zeros_like(acc_ref)

    n_chunks = s // tk

    def scores(c, dst_ref):
        kc = k_ref[pl.ds(pl.multiple_of(c * tk, tk), tk), :]
        dst_ref[...] = jnp.dot(kc, qt_ref[...], preferred_element_type=F32)

    def accumulate(c, src_ref):
        st = src_ref[...]
        m_old = m_ref[...]
        m_new = jnp.maximum(m_old, jnp.max(st, axis=0, keepdims=True))
        p = jnp.exp2(st - m_new)
        alpha = jnp.exp2(m_old - m_new)
        l_ref[...] = alpha * l_ref[...] + jnp.sum(p, axis=0, keepdims=True)
        acc_ref[...] = alpha * acc_ref[...] + jnp.dot(vt_ref[c], p.astype(BF16), preferred_element_type=F32)
        m_ref[...] = m_new

    scores(0, sa_ref)

    def body(i, carry):
        c = 2 * i
        scores(c + 1, sb_ref)
        accumulate(c, sa_ref)
        scores(jnp.minimum(c + 2, n_chunks - 1), sa_ref)
        accumulate(c + 1, sb_ref)
        return carry

    lax.fori_loop(0, n_chunks // 2, body, 0)
    out = acc_ref[...] / l_ref[...]
    tq = nq // ATT_REP
    heads = [out[:, r * tq:(r + 1) * tq] for r in range(ATT_REP)]
    o_ref[...] = jnp.concatenate(heads, axis=0).T.astype(o_ref.dtype)


def _attention(qt, k, vt, tq):
    s = k.shape[1]
    tk = vt.shape[2]
    nq = ATT_REP * tq
    return pl.pallas_call(
        functools.partial(_attn_kernel, tk=tk),
        grid=(ATT_KV_HEADS, s // tq),
        in_specs=[
            pl.BlockSpec((None, ATT_HEAD_DIM, nq), lambda g, i: (g, 0, i)),
            pl.BlockSpec((None, s, ATT_HEAD_DIM), lambda g, i: (g, 0, 0)),
            pl.BlockSpec((s // tk, ATT_HEAD_DIM, tk), lambda g, i: (0, g, 0)),
        ],
        out_specs=pl.BlockSpec((tq, ATT_REP * ATT_HEAD_DIM), lambda g, i: (i, g)),
        out_shape=jax.ShapeDtypeStruct((s, ATT_WIDTH), BF16),
        scratch_shapes=[pltpu.VMEM((1, nq), F32), pltpu.VMEM((1, nq), F32), pltpu.VMEM((ATT_HEAD_DIM, nq), F32),
                        pltpu.VMEM((tk, nq), F32), pltpu.VMEM((tk, nq), F32)],
        compiler_params=_params("parallel", "parallel"),
        name="attention",
    )(qt, k, vt)


def _ssd_out_kernel(yf_ref, yb_ref, xs_ref, z_ref, dskip_ref, g_ref, w_ref, o_ref):
    y = yf_ref[...] + yb_ref[...] + xs_ref[...] * dskip_ref[...]
    y = y * _silu(z_ref[...])
    ms = jnp.mean(y * y, axis=-1, keepdims=True)
    yn = (y * lax.rsqrt(ms + EPS) * g_ref[...]).astype(BF16)
    o_ref[...] = jnp.dot(yn, w_ref[...], preferred_element_type=F32)


def _ssd_out(yf, yb, xbc, proj, dskip, g, w, tm):
    s = yf.shape[0]
    row = lambda i: (i, 0)
    const = lambda i: (0, 0)
    return pl.pallas_call(
        _ssd_out_kernel,
        grid=(s // tm,),
        in_specs=[
            pl.BlockSpec((tm, SSD_INNER), row), pl.BlockSpec((tm, SSD_INNER), row),
            pl.BlockSpec((tm, SSD_INNER), row), pl.BlockSpec((tm, SSD_INNER), row),
            pl.BlockSpec((1, SSD_INNER), const), pl.BlockSpec((1, SSD_INNER), const),
            pl.BlockSpec((SSD_INNER, D_MODEL), const),
        ],
        out_specs=pl.BlockSpec((tm, D_MODEL), row),
        out_shape=jax.ShapeDtypeStruct((s, D_MODEL), F32),
        compiler_params=_params("parallel"),
        name="ssd_out",
    )(yf, yb, xbc, proj, dskip, g, w)


def _pool_kernel(um_ref, up_ref, un_ref, wp_ref, scale_ref, w_ref, o_ref, ext_ref, *, seq):
    tm = um_ref.shape[0]
    _fill_ext(ext_ref, um_ref, up_ref, un_ref, tm)
    t = pl.program_id(0) * tm + lax.broadcasted_iota(I32, (tm, 1), 0)
    mixed = []
    for gi, win in enumerate(POOL_WINDOWS):
        half = win // 2
        cols = slice(gi * POOL_GROUP_DIM, (gi + 1) * POOL_GROUP_DIM)
        acc = ext_ref[pl.ds(HALO - half, tm), cols]
        for k in range(1, win):
            acc = acc + ext_ref[pl.ds(HALO - half + k, tm), cols]
        cnt = (jnp.minimum(t + half, seq) - jnp.maximum(t - half, 0)).astype(F32)
        mean = acc / cnt
        mix = (mean - um_ref[:, cols]).astype(BF16)
        mixed.append(jnp.dot(mix, wp_ref[gi], preferred_element_type=F32))
    pooled = (jnp.concatenate(mixed, axis=1) * scale_ref[...]).astype(BF16)
    o_ref[...] = jnp.dot(pooled, w_ref[...], preferred_element_type=F32)


def _pool(proj, w_pool, scale, w_br, tm):
    s = proj.shape[0]
    return pl.pallas_call(
        functools.partial(_pool_kernel, seq=s),
        grid=(s // tm,),
        in_specs=_halo_specs(tm, POOL_WIDTH, COL_POOL // POOL_WIDTH, s) + [
            pl.BlockSpec((len(POOL_WINDOWS), POOL_GROUP_DIM, POOL_GROUP_DIM), lambda i: (0, 0, 0)),
            pl.BlockSpec((1, POOL_WIDTH), lambda i: (0, 0)),
            pl.BlockSpec((POOL_WIDTH, D_MODEL), lambda i: (0, 0)),
        ],
        out_specs=pl.BlockSpec((tm, D_MODEL), lambda i: (i, 0)),
        out_shape=jax.ShapeDtypeStruct((s, D_MODEL), F32),
        scratch_shapes=[pltpu.VMEM((tm + 2 * HALO, POOL_WIDTH), F32)],
        compiler_params=_params("parallel"),
        name="pool",
    )(proj, proj, proj, w_pool, scale, w_br)


def _merge_kernel(bs_ref, att_ref, bp_ref, g0_ref, g1_ref, g2_ref, gb_ref, x_ref, watt_ref, wout_ref, o_ref):
    b_att = jnp.dot(att_ref[...], watt_ref[...], preferred_element_type=F32)
    gb = gb_ref[...]
    merged = (jax.nn.sigmoid(g0_ref[...] + gb[0:1]) * bs_ref[...]
              + jax.nn.sigmoid(g1_ref[...] + gb[1:2]) * b_att
              + jax.nn.sigmoid(g2_ref[...] + gb[2:3]) * bp_ref[...])
    o_ref[...] = x_ref[...] + jnp.dot(merged.astype(BF16), wout_ref[...], preferred_element_type=F32)


def _merge(b_ssd, att, b_pool, proj, gate_bias, x, w_att, w_out, tm):
    s = x.shape[0]
    row = lambda i: (i, 0)
    const = lambda i: (0, 0)
    gcol = COL_GATES // D_MODEL
    tile = pl.BlockSpec((tm, D_MODEL), row)
    return pl.pallas_call(
        _merge_kernel,
        grid=(s // tm,),
        in_specs=[
            tile, tile, tile,
            pl.BlockSpec((tm, D_MODEL), lambda i: (i, gcol)),
            pl.BlockSpec((tm, D_MODEL), lambda i: (i, gcol + 1)),
            pl.BlockSpec((tm, D_MODEL), lambda i: (i, gcol + 2)),
            pl.BlockSpec((N_BRANCH, D_MODEL), const),
            tile,
            pl.BlockSpec((ATT_WIDTH, D_MODEL), const),
            pl.BlockSpec((D_MODEL, D_MODEL), const),
        ],
        out_specs=tile,
        out_shape=jax.ShapeDtypeStruct((s, D_MODEL), F32),
        compiler_params=_params("parallel"),
        name="merge",
    )(b_ssd, att, b_pool, proj, proj, proj, gate_bias, x, w_att, w_out)


ROUTER_GROUP_ROW = 0
ROUTER_EXPERT_ROW = SUBLANES
TILE_TABLE_LANES = 256


def _first_argmax(v, idx, big):
    m = jnp.max(v, axis=0, keepdims=True)
    return m, jnp.min(jnp.where(v == m, idx, big), axis=0, keepdims=True)


def _router_kernel(x_ref, g_ref, wt_ref, b_ref, upper_ref, ltri_ref,
                   h_ref, eidx_ref, rank_ref, wts_ref, off_ref, tiles_ref, count_ref, *, tile_rows):
    i = pl.program_id(0)
    tm = x_ref.shape[0]

    @pl.when(i == 0)
    def _():
        count_ref[...] = jnp.zeros_like(count_ref)

    x = x_ref[...]
    ms = jnp.mean(x * x, axis=-1, keepdims=True)
    h = x * lax.rsqrt(ms + EPS) * g_ref[...]
    h_ref[...] = h
    lt = lax.dot_general(wt_ref[...], h, (((1,), (1,)), ((), ())), precision=HIGHEST,
                         preferred_element_type=F32) + b_ref[...]

    r8 = lax.broadcasted_iota(I32, (SUBLANES, tm), 0)
    gl = jnp.where(r8 < MOE_GROUPS, lt[ROUTER_GROUP_ROW:ROUTER_GROUP_ROW + SUBLANES], -jnp.inf)
    gmax, gsel = _first_argmax(gl, r8, SUBLANES)
    p_group = 1.0 / jnp.sum(jnp.exp(gl - gmax), axis=0, keepdims=True)

    r32 = lax.broadcasted_iota(I32, (N_EXPERTS, tm), 0)
    el = lt[ROUTER_EXPERT_ROW:ROUTER_EXPERT_ROW + N_EXPERTS]
    el = jnp.where(lax.shift_right_logical(r32, int(math.log2(MOE_PER_GROUP))) == gsel, el, -jnp.inf)
    m1, i1 = _first_argmax(el, r32, N_EXPERTS)
    el2 = jnp.where(r32 == i1, -jnp.inf, el)
    m2, i2 = _first_argmax(el2, r32, N_EXPERTS)
    z = jnp.sum(jnp.exp(el - m1), axis=0, keepdims=True)
    p1 = 1.0 / z
    p2 = jnp.exp(m2 - m1) / z
    psum = p1 + p2
    wts_ref[0:1, :] = p_group * (p1 / psum)
    wts_ref[1:2, :] = p_group * (p2 / psum)
    eidx_ref[0:1, :] = i1
    eidx_ref[1:2, :] = i2

    sel0 = r32 == i1
    sel1 = r32 == i2
    member = jnp.where(sel0 | sel1, 1.0, 0.0)
    before = jnp.dot(member.astype(BF16), upper_ref[...], preferred_element_type=F32) + count_ref[:, 0:1]
    rank_ref[0:1, :] = jnp.sum(jnp.where(sel0, before, 0.0), axis=0, keepdims=True).astype(I32)
    rank_ref[1:2, :] = jnp.sum(jnp.where(sel1, before, 0.0), axis=0, keepdims=True).astype(I32)
    count_ref[...] = count_ref[...] + jnp.sum(member, axis=1, keepdims=True)

    @pl.when(i == pl.num_programs(0) - 1)
    def _():
        cnt = count_ref[...]
        ntile = jnp.floor((cnt + (tile_rows - 1)) * (1.0 / tile_rows))
        start = jnp.dot(ltri_ref[...], ntile.astype(BF16), preferred_element_type=F32)
        off_ref[...] = (start * tile_rows).astype(I32)
        end = (start + ntile)[:, 0:1]
        tid = lax.broadcasted_iota(I32, (N_EXPERTS, TILE_TABLE_LANES), 1).astype(F32)
        owner = jnp.sum(jnp.where(end <= tid, 1.0, 0.0), axis=0, keepdims=True)
        owner = jnp.minimum(owner, N_EXPERTS - 1.0)
        nact = jnp.sum(ntile[:, 0:1], axis=0, keepdims=True)
        r = lax.broadcasted_iota(I32, (SUBLANES, TILE_TABLE_LANES), 0)
        tiles_ref[...] = jnp.where(r == 0, owner, nact).astype(I32)


def _router(x, g, wt, b, upper, ltri, tm, tile_rows):
    s = x.shape[0]
    row = lambda i: (i, 0)
    const = lambda i: (0, 0)
    lane = lambda i: (0, i)
    return pl.pallas_call(
        functools.partial(_router_kernel, tile_rows=tile_rows),
        grid=(s // tm,),
        in_specs=[
            pl.BlockSpec((tm, D_MODEL), row),
            pl.BlockSpec((1, D_MODEL), const),
            pl.BlockSpec((LANES, D_MODEL), const),
            pl.BlockSpec((LANES, 1), const),
            pl.BlockSpec((tm, tm), const),
            pl.BlockSpec((N_EXPERTS, N_EXPERTS), const),
        ],
        out_specs=[
            pl.BlockSpec((tm, D_MODEL), row),
            pl.BlockSpec((2, tm), lane),
            pl.BlockSpec((2, tm), lane),
            pl.BlockSpec((2, tm), lane),
            pl.BlockSpec((N_EXPERTS, LANES), const),
            pl.BlockSpec((SUBLANES, TILE_TABLE_LANES), const),
        ],
        out_shape=[
            jax.ShapeDtypeStruct((s, D_MODEL), F32),
            jax.ShapeDtypeStruct((2, s), I32),
            jax.ShapeDtypeStruct((2, s), I32),
            jax.ShapeDtypeStruct((2, s), F32),
            jax.ShapeDtypeStruct((N_EXPERTS, LANES), I32),
            jax.ShapeDtypeStruct((SUBLANES, TILE_TABLE_LANES), I32),
        ],
        scratch_shapes=[pltpu.VMEM((N_EXPERTS, LANES), F32)],
        compiler_params=_params("arbitrary"),
        name="router",
    )(x, g, wt, b, upper, ltri)


def _sorted_row(eidx_s, rank_s, off_s, slot, tok, seq):
    return off_s[eidx_s[slot * seq + tok]] + rank_s[slot * seq + tok]


def _dispatch_kernel(eidx_s, rank_s, off_s, h_ref, xs_in_hbm, xs_hbm, sem, *, seq):
    del xs_in_hbm
    tile = h_ref.shape[0]
    base = pl.program_id(0) * tile

    def row_copy(t, slot):
        pos = _sorted_row(eidx_s, rank_s, off_s, slot, base + t, seq)
        return pltpu.make_async_copy(h_ref.at[pl.ds(t, 1), :], xs_hbm.at[pl.ds(pos, 1), :], sem)

    def start(t, c):
        row_copy(t, 0).start()
        row_copy(t, 1).start()
        return c

    def wait(t, c):
        row_copy(t, 0).wait()
        row_copy(t, 1).wait()
        return c

    lax.fori_loop(0, tile, start, 0)
    lax.fori_loop(0, tile, wait, 0)


def _dispatch(eidx, rank, off, h, xs_zero, tile):
    s = h.shape[0]
    return pl.pallas_call(
        functools.partial(_dispatch_kernel, seq=s),
        grid_spec=pltpu.PrefetchScalarGridSpec(
            num_scalar_prefetch=3,
            grid=(s // tile,),
            in_specs=[pl.BlockSpec((tile, D_MODEL), lambda i, *_: (i, 0)), pl.BlockSpec(memory_space=pl.ANY)],
            out_specs=pl.BlockSpec(memory_space=pl.ANY),
            scratch_shapes=[pltpu.SemaphoreType.DMA],
        ),
        out_shape=jax.ShapeDtypeStruct(xs_zero.shape, xs_zero.dtype),
        input_output_aliases={4: 0},
        compiler_params=_params("arbitrary"),
        name="dispatch",
    )(eidx, rank, off, h, xs_zero)


def _ffn_kernel(owner_s, nact_s, x_ref, wg_ref, wu_ref, wd_ref, o_ref):
    j = pl.program_id(0)

    @pl.when(j < nact_s[0])
    def _():
        x = x_ref[...].astype(BF16)
        gate = jnp.dot(x, wg_ref[...], preferred_element_type=F32)
        up = jnp.dot(x, wu_ref[...], preferred_element_type=F32)
        hid = (_silu(gate) * up).astype(BF16)
        o_ref[...] = jnp.dot(hid, wd_ref[...], preferred_element_type=F32)

    @pl.when(j >= nact_s[0])
    def _():
        o_ref[...] = jnp.zeros_like(o_ref)


def _expert_ffn(owner, nact, xs, wg, wu, wd, tile_rows):
    rows = xs.shape[0]
    return pl.pallas_call(
        _ffn_kernel,
        grid_spec=pltpu.PrefetchScalarGridSpec(
            num_scalar_prefetch=2,
            grid=(rows // tile_rows,),
            in_specs=[
                pl.BlockSpec((tile_rows, D_MODEL), lambda j, o, n: (j, 0)),
                pl.BlockSpec((None, D_MODEL, EXPERT_FF), lambda j, o, n: (o[j], 0, 0)),
                pl.BlockSpec((None, D_MODEL, EXPERT_FF), lambda j, o, n: (o[j], 0, 0)),
                pl.BlockSpec((None, EXPERT_FF, D_MODEL), lambda j, o, n: (o[j], 0, 0)),
            ],
            out_specs=pl.BlockSpec((tile_rows, D_MODEL), lambda j, o, n: (j, 0)),
        ),
        out_shape=jax.ShapeDtypeStruct((rows, D_MODEL), F32),
        compiler_params=_params("arbitrary"),
        name="expert_ffn",
    )(owner, nact, xs, wg, wu, wd)


def _combine_kernel(eidx_s, rank_s, off_s, ys_hbm, x_ref, w_ref, o_ref, buf_ref, sem, *, seq):
    tile = x_ref.shape[0]
    base = pl.program_id(0) * tile

    def row_copy(t, slot):
        pos = _sorted_row(eidx_s, rank_s, off_s, slot, base + t, seq)
        return pltpu.make_async_copy(ys_hbm.at[pl.ds(pos, 1), :], buf_ref.at[slot, pl.ds(t, 1), :], sem)

    def start(t, c):
        row_copy(t, 0).start()
        row_copy(t, 1).start()
        return c

    def wait(t, c):
        row_copy(t, 0).wait()
        row_copy(t, 1).wait()
        return c

    lax.fori_loop(0, tile, start, 0)
    lax.fori_loop(0, tile, wait, 0)
    w = w_ref[...]
    o_ref[...] = x_ref[...] + w[:, 0:1] * buf_ref[0] + w[:, 1:2] * buf_ref[1]


def _combine(eidx, rank, off, ys, x, w_cols, tile):
    s = x.shape[0]
    return pl.pallas_call(
        functools.partial(_combine_kernel, seq=s),
        grid_spec=pltpu.PrefetchScalarGridSpec(
            num_scalar_prefetch=3,
            grid=(s // tile,),
            in_specs=[
                pl.BlockSpec(memory_space=pl.ANY),
                pl.BlockSpec((tile, D_MODEL), lambda i, *_: (i, 0)),
                pl.BlockSpec((tile, 2), lambda i, *_: (i, 0)),
            ],
            out_specs=pl.BlockSpec((tile, D_MODEL), lambda i, *_: (i, 0)),
            scratch_shapes=[pltpu.VMEM((2, tile, D_MODEL), F32), pltpu.SemaphoreType.DMA],
        ),
        out_shape=jax.ShapeDtypeStruct((s, D_MODEL), F32),
        compiler_params=_params("arbitrary"),
        name="combine",
    )(eidx, rank, off, ys, x, w_cols)


def _tiles(s):
    return dict(
        inproj_tm=min(1024, s), inproj_tn=768,
        conv_tm=min(512, s), prep_tm=min(1024, s),
        attn_tq=256, attn_tk=256,
        tail_tm=min(256, s), pool_tm=min(512, s),
        router_tm=min(512, s), moe_rows=256, dispatch_tile=min(512, s), combine_tile=min(256, s),
    )


def _rope_tables(s):
    rows = s // GRID_W
    row = jnp.repeat(jnp.arange(rows), GRID_W).astype(F32)
    col = jnp.tile(jnp.arange(GRID_W), rows).astype(F32)
    half = ATT_HEAD_DIM // 2
    inv_freq = ROPE_THETA ** (-jnp.arange(0, half, 2, dtype=F32) / half)
    ang = jnp.concatenate([row[:, None] * inv_freq, col[:, None] * inv_freq], axis=-1)
    d = jnp.arange(LANES)
    idx = ((d % ATT_HEAD_DIM) // half) * (half // 2) + d % (half // 2)
    return jnp.cos(ang)[:, idx], jnp.sin(ang)[:, idx]


def _head_constants():
    d = jnp.arange(LANES)
    same_head = (d[:, None] // ATT_HEAD_DIM) == (d[None, :] // ATT_HEAD_DIM)
    bd = same_head.astype(BF16)
    quarter = ATT_HEAD_DIM // 4
    second = (d % (2 * quarter)) >= quarter
    src = jnp.where(second, d - quarter, d + quarter)
    sign = jnp.where(second, 1.0, -1.0)
    perm = jnp.zeros((LANES, LANES), F32).at[src, d].set(sign).astype(BF16)
    return bd, perm


def _expand_matrix(lane0):
    r = jnp.arange(LANES)[:, None]
    c = jnp.arange(SSD_INNER)[None, :]
    return (r == lane0 + c // SSD_HEAD_DIM).astype(BF16)


def _pad_lanes(v, width=LANES):
    flat = v.reshape(1, -1)
    return jnp.pad(flat, ((0, 0), (0, width - flat.shape[1])))


def _mixer_layer(x, t, cos, sin, bd, perm, e_f, e_b, norm_mix, w_in, conv_w, conv_b, dt_bias, a_log, d_skip,
                 ssd_norm, w_br_ssd, q_norm, k_norm, w_br_att, w_pool, pool_scale, w_br_pool, gate_bias, w_out):
    o_z, o_xbc = 0, SSD_INNER
    o_dt = o_xbc + SSD_CONV_DIM
    o_q = o_dt + 2 * SSD_HEADS
    o_kv = o_q + ATT_WIDTH
    o_pool = o_kv + 2 * ATT_KV_WIDTH
    o_gate = o_pool + POOL_WIDTH
    w_main = jnp.concatenate(
        [w_in[:, o_z:o_dt], w_in[:, o_gate:], w_in[:, o_q:o_kv], w_in[:, o_pool:o_gate], w_in[:, o_kv:o_pool]],
        axis=1).astype(BF16)
    w_dt = jnp.pad(w_in[:, o_dt:o_q], ((0, 0), (0, DT_PAD - 2 * SSD_HEADS))).astype(BF16)

    proj, dt_raw = _in_proj(x, norm_mix.reshape(1, -1), w_main, w_dt, t["inproj_tm"], t["inproj_tn"])

    xbc = _conv(proj, conv_w, conv_b.reshape(1, -1), t["conv_tm"])
    y_f, y_b = _ssd(xbc, dt_raw, _pad_lanes(dt_bias), _pad_lanes(a_log), e_f, e_b)
    b_ssd = _ssd_out(y_f, y_b, xbc, proj, jnp.repeat(d_skip, SSD_HEAD_DIM).reshape(1, -1),
                     ssd_norm.reshape(1, -1), w_br_ssd.astype(BF16), t["tail_tm"])

    qn = jnp.tile(q_norm, LANES // ATT_HEAD_DIM).reshape(1, -1)
    kn = jnp.tile(k_norm, LANES // ATT_HEAD_DIM).reshape(1, -1)
    qt = _q_prep(proj, cos, sin, qn, bd, perm, t["attn_tq"])
    k = _k_prep(proj, cos, sin, kn, bd, perm, t["prep_tm"])
    vt = _v_t(proj, t["attn_tk"])
    att = _attention(qt, k, vt, t["attn_tq"])

    b_pool = _pool(proj, w_pool.astype(BF16), pool_scale.reshape(1, -1), w_br_pool.astype(BF16), t["pool_tm"])
    return _merge(b_ssd, att, b_pool, proj, gate_bias, x, w_br_att.astype(BF16), w_out.astype(BF16), t["tail_tm"])


def _moe_layer(x, t, upper, ltri, norm_ffn, w_rg, b_rg, w_re, b_re, w_gate_e, w_up_e, w_down_e):
    s = x.shape[0]
    rows_per_tile = t["moe_rows"]
    wt = jnp.zeros((LANES, D_MODEL), F32)
    wt = wt.at[ROUTER_GROUP_ROW:ROUTER_GROUP_ROW + MOE_GROUPS].set(w_rg.T)
    wt = wt.at[ROUTER_EXPERT_ROW:ROUTER_EXPERT_ROW + N_EXPERTS].set(w_re.T)
    bias = jnp.zeros((LANES, 1), F32)
    bias = bias.at[ROUTER_GROUP_ROW:ROUTER_GROUP_ROW + MOE_GROUPS, 0].set(b_rg)
    bias = bias.at[ROUTER_EXPERT_ROW:ROUTER_EXPERT_ROW + N_EXPERTS, 0].set(b_re)

    h, eidx, rank, wts, off, tiles = _router(x, norm_ffn.reshape(1, -1), wt, bias, upper, ltri,
                                             t["router_tm"], rows_per_tile)
    eidx, rank, off = eidx.reshape(-1), rank.reshape(-1), off[:, 0]
    sorted_rows = 2 * s + N_EXPERTS * rows_per_tile
    xs = _dispatch(eidx, rank, off, h, jnp.zeros((sorted_rows, D_MODEL), F32), t["dispatch_tile"])
    ys = _expert_ffn(tiles[0], tiles[1, 0:1], xs, w_gate_e.astype(BF16), w_up_e.astype(BF16),
                     w_down_e.astype(BF16), rows_per_tile)
    return _combine(eidx, rank, off, ys, x, wts.T, t["combine_tile"])


def kernel(x, norm_mix, w_in, conv_w, conv_b, dt_bias, a_log, d_skip, ssd_norm, w_br_ssd, q_norm, k_norm, w_br_att, w_pool, pool_scale, w_br_pool, gate_bias, w_out, norm_ffn, w_router_group, b_router_group, w_router_expert, b_router_expert, w_gate_e, w_up_e, w_down_e):
    b, s, _ = x.shape
    assert b == 1 and s % 1024 == 0 and s % GRID_W == 0
    t = _tiles(s)
    assert 2 * s // t["moe_rows"] + N_EXPERTS <= TILE_TABLE_LANES
    cos, sin = _rope_tables(s)
    bd, perm = _head_constants()
    e_f, e_b = _expand_matrix(0), _expand_matrix(SSD_HEADS)
    rt = t["router_tm"]
    upper = (jnp.arange(rt)[:, None] < jnp.arange(rt)[None, :]).astype(BF16)
    ltri = (jnp.arange(N_EXPERTS)[None, :] < jnp.arange(N_EXPERTS)[:, None]).astype(BF16)

    xx = x[0]
    for l in range(DEPTH):
        xx = _mixer_layer(xx, t, cos, sin, bd, perm, e_f, e_b, norm_mix[l], w_in[l], conv_w[l], conv_b[l], dt_bias[l],
                          a_log[l], d_skip[l], ssd_norm[l], w_br_ssd[l], q_norm[l], k_norm[l], w_br_att[l],
                          w_pool[l], pool_scale[l], w_br_pool[l], gate_bias[l], w_out[l])
        xx = _moe_layer(xx, t, upper, ltri, norm_ffn[l], w_router_group[l], b_router_group[l],
                        w_router_expert[l], b_router_expert[l], w_gate_e[l], w_up_e[l], w_down_e[l])
    return xx[None]
```

```python
import functools
import math

import jax
import jax.numpy as jnp
from jax import lax
from jax.experimental import pallas as pl
from jax.experimental.pallas import tpu as pltpu

F32 = jnp.float32
BF16 = jnp.bfloat16
I32 = jnp.int32
HIGHEST = lax.Precision.HIGHEST

EPS = 1e-6
D_MODEL = 1024
DEPTH = 2

SSD_INNER = 2048
SSD_HEAD_DIM = 64
SSD_HEADS = 32
SSD_GROUPS = 4
SSD_STATE = 128
SSD_CONV = 5
SSD_CHUNK = 128
SSD_BC = SSD_GROUPS * SSD_STATE
SSD_CONV_DIM = SSD_INNER + 2 * SSD_BC
HEADS_PER_GROUP = SSD_HEADS // SSD_GROUPS
GROUP_WIDTH = HEADS_PER_GROUP * SSD_HEAD_DIM

ATT_HEADS = 16
ATT_KV_HEADS = 4
ATT_HEAD_DIM = 64
ATT_WIDTH = 1024
ATT_KV_WIDTH = 256
ATT_REP = ATT_HEADS // ATT_KV_HEADS
ROPE_THETA = 10000.0
GRID_W = 64

POOL_WIDTH = 1024
POOL_WINDOWS = (2, 4, 8, 16)
POOL_GROUP_DIM = 256

N_BRANCH = 3
MOE_GROUPS = 4
MOE_PER_GROUP = 8
N_EXPERTS = 32
EXPERT_FF = 512

LANES = 128
SUBLANES = 8
VMEM_LIMIT_BYTES = 56 * 1024 * 1024

COL_Z = 0
COL_XBC = COL_Z + SSD_INNER
COL_GATES = COL_XBC + SSD_CONV_DIM
COL_Q = COL_GATES + N_BRANCH * D_MODEL
COL_POOL = COL_Q + ATT_WIDTH
COL_K = COL_POOL + POOL_WIDTH
COL_V = COL_K + ATT_KV_WIDTH
PROJ_DIM = COL_V + ATT_KV_WIDTH
DT_PAD = LANES

HALO = 2 * SUBLANES


def _params(*sem):
    return pltpu.CompilerParams(dimension_semantics=sem, vmem_limit_bytes=VMEM_LIMIT_BYTES)


def _silu(v):
    return v * jax.nn.sigmoid(v)


def _split_bf16(v):
    hi = v.astype(BF16)
    lo = (v - hi.astype(F32)).astype(BF16)
    return hi, lo


def _inproj_kernel(x_ref, g_ref, w_ref, wdt_ref, o_ref, dt_ref, h_ref):
    @pl.when(pl.program_id(1) == 0)
    def _():
        x = x_ref[...]
        ms = jnp.mean(x * x, axis=-1, keepdims=True)
        h = (x * lax.rsqrt(ms + EPS) * g_ref[...]).astype(BF16)
        h_ref[...] = h
        dt_ref[...] = jnp.dot(h, wdt_ref[...], preferred_element_type=F32)

    o_ref[...] = jnp.dot(h_ref[...], w_ref[...], preferred_element_type=F32).astype(o_ref.dtype)


def _in_proj(x, g, w, wdt, tm, tn):
    s = x.shape[0]
    return pl.pallas_call(
        _inproj_kernel,
        grid=(s // tm, PROJ_DIM // tn),
        in_specs=[
            pl.BlockSpec((tm, D_MODEL), lambda i, j: (i, 0)),
            pl.BlockSpec((1, D_MODEL), lambda i, j: (0, 0)),
            pl.BlockSpec((D_MODEL, tn), lambda i, j: (0, j)),
            pl.BlockSpec((D_MODEL, DT_PAD), lambda i, j: (0, 0)),
        ],
        out_specs=[
            pl.BlockSpec((tm, tn), lambda i, j: (i, j)),
            pl.BlockSpec((tm, DT_PAD), lambda i, j: (i, 0)),
        ],
        out_shape=[jax.ShapeDtypeStruct((s, PROJ_DIM), BF16), jax.ShapeDtypeStruct((s, DT_PAD), F32)],
        scratch_shapes=[pltpu.VMEM((tm, D_MODEL), BF16)],
        compiler_params=_params("parallel", "arbitrary"),
        name="in_proj",
    )(x, g, w, wdt)


def _fill_ext(ext_ref, main_ref, prev_ref, next_ref, tm):
    i = pl.program_id(0)
    last = pl.num_programs(0) - 1
    ext_ref[0:HALO, :] = jnp.where(i > 0, prev_ref[...].astype(F32), 0.0)
    ext_ref[HALO:HALO + tm, :] = main_ref[...].astype(F32)
    ext_ref[HALO + tm:, :] = jnp.where(i < last, next_ref[...].astype(F32), 0.0)


def _conv_kernel(xm_ref, xp_ref, xn_ref, w_ref, b_ref, o_ref, ext_ref):
    tm = xm_ref.shape[0]
    _fill_ext(ext_ref, xm_ref, xp_ref, xn_ref, tm)
    pad = SSD_CONV // 2
    acc = jnp.broadcast_to(b_ref[...], o_ref.shape)
    for k in range(SSD_CONV):
        acc = acc + ext_ref[pl.ds(HALO - pad + k, tm), :] * w_ref[k:k + 1, :]
    o_ref[...] = _silu(acc).astype(o_ref.dtype)


def _halo_specs(tm, tc, col0, s):
    rb = tm // HALO
    nrb = s // HALO
    return [
        pl.BlockSpec((tm, tc), lambda i, j=0: (i, col0 + j)),
        pl.BlockSpec((HALO, tc), lambda i, j=0: (jnp.maximum(i * rb - 1, 0), col0 + j)),
        pl.BlockSpec((HALO, tc), lambda i, j=0: (jnp.minimum((i + 1) * rb, nrb - 1), col0 + j)),
    ]


def _conv(proj, w, b, tm):
    s = proj.shape[0]
    tc = 1024
    return pl.pallas_call(
        _conv_kernel,
        grid=(s // tm, SSD_CONV_DIM // tc),
        in_specs=_halo_specs(tm, tc, COL_XBC // tc, s) + [
            pl.BlockSpec((SSD_CONV, tc), lambda i, j: (0, j)),
            pl.BlockSpec((1, tc), lambda i, j: (0, j)),
        ],
        out_specs=pl.BlockSpec((tm, tc), lambda i, j: (i, j)),
        out_shape=jax.ShapeDtypeStruct((s, SSD_CONV_DIM), BF16),
        scratch_shapes=[pltpu.VMEM((tm + 2 * HALO, tc), F32)],
        compiler_params=_params("parallel", "parallel"),
        name="conv",
    )(proj, proj, proj, w, b)


def _ssd_direction(x_ref, b_ref, c_ref, dt_ref, e_ref, y_ref, st_ref, bias, a, fwd):
    q = SSD_CHUNK
    row = lax.broadcasted_iota(I32, (q, q), 0)
    col = lax.broadcasted_iota(I32, (q, q), 1)
    mask = (col <= row) if fwd else (col >= row)
    lane0 = 0 if fwd else SSD_HEADS

    dt = jax.nn.softplus(dt_ref[...] + bias)
    da = dt * a
    cs = jnp.dot(mask.astype(F32), da, precision=HIGHEST, preferred_element_type=F32)
    cs_t = cs.T
    edge = q - 1 if fwd else 0
    tot = cs[edge:edge + 1, :]
    dt_hi, dt_lo = _split_bf16(dt)
    dec = jnp.exp(tot)
    dec_hi = dec.astype(BF16).astype(F32)
    r16 = lax.broadcasted_iota(I32, (2 * SUBLANES, LANES), 0)
    dec_rows = jnp.where(r16 == 0, dec_hi, jnp.where(r16 == 1, dec - dec_hi, 0.0)).astype(BF16)
    lhs = jnp.concatenate(
        [dt_hi, dt_lo, jnp.exp(cs).astype(BF16), jnp.exp(tot - cs).astype(BF16), dec_rows], axis=0)
    ex = jnp.dot(lhs, e_ref[...], preferred_element_type=F32)
    dt_x = ex[0:q] + ex[q:2 * q]
    ecs_x = ex[2 * q:3 * q]
    dte_x = ex[3 * q:4 * q]
    dec_x = ex[4 * q:4 * q + 1] + ex[4 * q + 1:4 * q + 2]

    xdt = x_ref[...].astype(F32) * dt_x
    xdte_b = (xdt * dte_x).astype(BF16)
    lane = lax.broadcasted_iota(I32, (q, LANES), 1)
    first_half = lane < SSD_HEAD_DIM
    bm = b_ref[...].astype(BF16)
    cm = c_ref[...].astype(BF16)
    for g in range(SSD_GROUPS):
        bg = bm[:, g * SSD_STATE:(g + 1) * SSD_STATE]
        cg = cm[:, g * SSD_STATE:(g + 1) * SSD_STATE]
        cb = lax.dot_general(cg, bg, (((1,), (1,)), ((), ())), preferred_element_type=F32)
        gs = slice(g * GROUP_WIDTH, (g + 1) * GROUP_WIDTH)
        st = st_ref[g]
        y_off = jnp.dot(cg, st.astype(BF16), preferred_element_type=F32) * ecs_x[:, gs]
        for pr in range(HEADS_PER_GROUP // 2):
            c0 = g * GROUP_WIDTH + pr * LANES
            xp = xdt[:, c0:c0 + LANES]
            halves = (jnp.where(first_half, xp, 0.0).astype(BF16), jnp.where(first_half, 0.0, xp).astype(BF16))
            yd = None
            for k in range(2):
                hl = lane0 + g * HEADS_PER_GROUP + 2 * pr + k
                seg = cs[:, hl:hl + 1] - cs_t[hl:hl + 1, :]
                decay = jnp.exp(jnp.where(mask, seg, -jnp.inf))
                part = jnp.dot((cb * decay).astype(BF16), halves[k], preferred_element_type=F32)
                yd = part if yd is None else yd + part
            y_ref[:, c0:c0 + LANES] = (yd + y_off[:, pr * LANES:(pr + 1) * LANES]).astype(y_ref.dtype)
        upd = lax.dot_general(bg, xdte_b[:, gs], (((0,), (0,)), ((), ())), preferred_element_type=F32)
        st_ref[g] = st * dec_x[:, gs] + upd


def _ssd_kernel(xf_ref, bf_ref, cf_ref, dtf_ref, xb_ref, bb_ref, cb_ref, dtb_ref,
                bias_ref, alog_ref, ef_ref, eb_ref, yf_ref, yb_ref, stf_ref, stb_ref):
    @pl.when(pl.program_id(0) == 0)
    def _():
        stf_ref[...] = jnp.zeros_like(stf_ref)
        stb_ref[...] = jnp.zeros_like(stb_ref)

    bias = bias_ref[...]
    a = -jnp.exp(alog_ref[...])
    _ssd_direction(xf_ref, bf_ref, cf_ref, dtf_ref, ef_ref, yf_ref, stf_ref, bias, a, True)
    _ssd_direction(xb_ref, bb_ref, cb_ref, dtb_ref, eb_ref, yb_ref, stb_ref, bias, a, False)


def _ssd(xbc, dt_raw, dt_bias, a_log, e_f, e_b):
    s = xbc.shape[0]
    q = SSD_CHUNK
    nc = s // q
    nb = SSD_INNER // SSD_BC

    def chunk_specs(im):
        return [
            pl.BlockSpec((q, SSD_INNER), lambda i: (im(i), 0)),
            pl.BlockSpec((q, SSD_BC), lambda i: (im(i), nb)),
            pl.BlockSpec((q, SSD_BC), lambda i: (im(i), nb + 1)),
            pl.BlockSpec((q, DT_PAD), lambda i: (im(i), 0)),
        ]

    fwd = lambda i: i
    bwd = lambda i: nc - 1 - i
    const = lambda shape: pl.BlockSpec(shape, lambda i: (0,) * len(shape))
    st_shape = (SSD_GROUPS, SSD_STATE, GROUP_WIDTH)
    return pl.pallas_call(
        _ssd_kernel,
        grid=(nc,),
        in_specs=chunk_specs(fwd) + chunk_specs(bwd) + [
            const((1, DT_PAD)), const((1, DT_PAD)), const((LANES, SSD_INNER)), const((LANES, SSD_INNER))],
        out_specs=[pl.BlockSpec((q, SSD_INNER), lambda i: (fwd(i), 0)),
                   pl.BlockSpec((q, SSD_INNER), lambda i: (bwd(i), 0))],
        out_shape=[jax.ShapeDtypeStruct((s, SSD_INNER), BF16)] * 2,
        scratch_shapes=[pltpu.VMEM(st_shape, F32), pltpu.VMEM(st_shape, F32)],
        compiler_params=_params("arbitrary"),
        name="ssd",
    )(xbc, xbc, xbc, dt_raw, xbc, xbc, xbc, dt_raw, dt_bias, a_log, e_f, e_b)


def _norm_rope(x, cos, sin, g, bd, perm):
    hi, lo = _split_bf16(x * x)
    ss = jnp.dot(hi, bd, preferred_element_type=F32) + jnp.dot(lo, bd, preferred_element_type=F32)
    y = x * lax.rsqrt(ss * (1.0 / ATT_HEAD_DIM) + EPS) * g
    yh, yl = _split_bf16(y)
    rot = jnp.dot(yh, perm, preferred_element_type=F32) + jnp.dot(yl, perm, preferred_element_type=F32)
    return y * cos + rot * sin


def _qprep_kernel(x_ref, cos_ref, sin_ref, g_ref, bd_ref, perm_ref, o_ref, *, scale):
    tq = x_ref.shape[0]
    cos, sin, g, bd, perm = cos_ref[...], sin_ref[...], g_ref[...], bd_ref[...], perm_ref[...]
    for j in range(ATT_WIDTH // LANES):
        out = _norm_rope(x_ref[:, j * LANES:(j + 1) * LANES].astype(F32), cos, sin, g, bd, perm) * scale
        out_t = out.T.astype(BF16)
        grp, r0 = (2 * j) // ATT_REP, (2 * j) % ATT_REP
        o_ref[grp, :, r0 * tq:(r0 + 1) * tq] = out_t[:ATT_HEAD_DIM]
        o_ref[grp, :, (r0 + 1) * tq:(r0 + 2) * tq] = out_t[ATT_HEAD_DIM:]


def _kprep_kernel(x_ref, cos_ref, sin_ref, g_ref, bd_ref, perm_ref, o_ref):
    out = _norm_rope(x_ref[...].astype(F32), cos_ref[...], sin_ref[...], g_ref[...], bd_ref[...],
                     perm_ref[...]).astype(BF16)
    o_ref[0] = out[:, :ATT_HEAD_DIM]
    o_ref[1] = out[:, ATT_HEAD_DIM:]


VT_ROWS = ATT_HEAD_DIM + 2 * SUBLANES


def _vt_kernel(x_ref, o_ref):
    xt = x_ref[...].astype(F32).T.astype(BF16)
    pad = VT_ROWS - ATT_HEAD_DIM
    ones_row = (lax.broadcasted_iota(I32, (pad, xt.shape[1]), 0) == 0).astype(BF16)
    for g in range(ATT_KV_HEADS):
        o_ref[g * VT_ROWS:g * VT_ROWS + ATT_HEAD_DIM, :] = xt[g * ATT_HEAD_DIM:(g + 1) * ATT_HEAD_DIM]
        o_ref[g * VT_ROWS + ATT_HEAD_DIM:(g + 1) * VT_ROWS, :] = ones_row


def _rope_specs(tm):
    return [
        pl.BlockSpec((tm, LANES), lambda i, j: (i, 0)),
        pl.BlockSpec((tm, LANES), lambda i, j: (i, 0)),
        pl.BlockSpec((1, LANES), lambda i, j: (0, 0)),
        pl.BlockSpec((LANES, LANES), lambda i, j: (0, 0)),
        pl.BlockSpec((LANES, LANES), lambda i, j: (0, 0)),
    ]


def _q_prep(proj, cos, sin, g, bd, perm, tq):
    s = proj.shape[0]
    scale = ATT_HEAD_DIM ** -0.5 * math.log2(math.e)
    const = lambda i: (0, 0)
    return pl.pallas_call(
        functools.partial(_qprep_kernel, scale=scale),
        grid=(s // tq,),
        in_specs=[
            pl.BlockSpec((tq, ATT_WIDTH), lambda i: (i, COL_Q // ATT_WIDTH)),
            pl.BlockSpec((tq, LANES), lambda i: (i, 0)),
            pl.BlockSpec((tq, LANES), lambda i: (i, 0)),
            pl.BlockSpec((1, LANES), const),
            pl.BlockSpec((LANES, LANES), const),
            pl.BlockSpec((LANES, LANES), const),
        ],
        out_specs=pl.BlockSpec((ATT_KV_HEADS, ATT_HEAD_DIM, ATT_REP * tq), lambda i: (0, 0, i)),
        out_shape=jax.ShapeDtypeStruct((ATT_KV_HEADS, ATT_HEAD_DIM, ATT_REP * s), BF16),
        compiler_params=_params("parallel"),
        name="q_prep",
    )(proj, cos, sin, g, bd, perm)


def _k_prep(proj, cos, sin, g, bd, perm, tm):
    s = proj.shape[0]
    return pl.pallas_call(
        _kprep_kernel,
        grid=(s // tm, ATT_KV_WIDTH // LANES),
        in_specs=[pl.BlockSpec((tm, LANES), lambda i, j: (i, COL_K // LANES + j))] + _rope_specs(tm),
        out_specs=pl.BlockSpec((2, tm, ATT_HEAD_DIM), lambda i, j: (j, i, 0)),
        out_shape=jax.ShapeDtypeStruct((ATT_KV_HEADS, s, ATT_HEAD_DIM), BF16),
        compiler_params=_params("parallel", "parallel"),
        name="k_prep",
    )(proj, cos, sin, g, bd, perm)


def _v_t(proj, tk):
    s = proj.shape[0]
    rows = ATT_KV_HEADS * VT_ROWS
    return pl.pallas_call(
        _vt_kernel,
        grid=(s // tk,),
        in_specs=[pl.BlockSpec((tk, ATT_KV_WIDTH), lambda i: (i, COL_V // ATT_KV_WIDTH))],
        out_specs=pl.BlockSpec((None, rows, tk), lambda i: (i, 0, 0)),
        out_shape=jax.ShapeDtypeStruct((s // tk, rows, tk), BF16),
        compiler_params=_params("parallel"),
        name="v_t",
    )(proj)


SAFE_SCORE_SPAN = 100.0


def _attn_kernel(qt_ref, k_ref, vt_ref, o_ref, kmax_ref, shift_ref, acc_ref, sa_ref, sb_ref, *, tk):
    s = k_ref.shape[0]
    nq = qt_ref.shape[1]
    n_chunks = s // tk

    @pl.when(pl.program_id(1) == 0)
    def _():
        def key_norm(c, best):
            kc = k_ref[pl.ds(pl.multiple_of(c * tk, tk), tk), :].astype(F32)
            norm = jnp.sum(kc * kc, axis=1, keepdims=True)
            return jnp.maximum(best, jnp.max(norm, axis=0, keepdims=True))

        best = lax.fori_loop(0, n_chunks, key_norm, jnp.zeros((1, 1), F32))
        kmax_ref[...] = jnp.broadcast_to(best, kmax_ref.shape)

    q = qt_ref[...].astype(F32)
    bound = jnp.sqrt(jnp.sum(q * q, axis=0, keepdims=True) * kmax_ref[0:1, 0:1])
    safe = 2.0 * jnp.max(bound) <= SAFE_SCORE_SPAN
    acc_ref[...] = jnp.zeros_like(acc_ref)

    def scores(c, dst_ref):
        kc = k_ref[pl.ds(pl.multiple_of(c * tk, tk), tk), :]
        dst_ref[...] = jnp.dot(kc, qt_ref[...], preferred_element_type=F32)

    def sweep(accumulate, unroll):
        scores(0, sa_ref)

        def body(i, carry):
            c = 2 * i
            scores(c + 1, sb_ref)
            accumulate(c, sa_ref)
            scores(jnp.minimum(c + 2, n_chunks - 1), sa_ref)
            accumulate(c + 1, sb_ref)
            return carry

        lax.fori_loop(0, n_chunks // 2, body, 0, unroll=unroll)

    @pl.when(safe)
    def _():
        def accumulate(c, src_ref):
            p = jnp.exp2(src_ref[...] - bound).astype(BF16)
            acc_ref[...] += jnp.dot(vt_ref[c], p, preferred_element_type=F32)

        sweep(accumulate, unroll=4)

    @pl.when(jnp.logical_not(safe))
    def _():
        shift_ref[...] = jnp.full_like(shift_ref, -jnp.inf)

        def accumulate(c, src_ref):
            st = src_ref[...]
            m_old = shift_ref[...]
            m_new = jnp.maximum(m_old, jnp.max(st, axis=0, keepdims=True))
            p = jnp.exp2(st - m_new).astype(BF16)
            acc_ref[...] = jnp.exp2(m_old - m_new) * acc_ref[...] + jnp.dot(vt_ref[c], p, preferred_element_type=F32)
            shift_ref[...] = m_new

        sweep(accumulate, unroll=1)

    acc = acc_ref[...]
    out = acc[:ATT_HEAD_DIM] / acc[ATT_HEAD_DIM:ATT_HEAD_DIM + 1]
    tq = nq // ATT_REP
    heads = [out[:, r * tq:(r + 1) * tq] for r in range(ATT_REP)]
    o_ref[...] = jnp.concatenate(heads, axis=0).T.astype(o_ref.dtype)


def _attention(qt, k, vt, tq):
    s = k.shape[1]
    tk = vt.shape[2]
    nq = ATT_REP * tq
    return pl.pallas_call(
        functools.partial(_attn_kernel, tk=tk),
        grid=(ATT_KV_HEADS, s // tq),
        in_specs=[
            pl.BlockSpec((None, ATT_HEAD_DIM, nq), lambda g, i: (g, 0, i)),
            pl.BlockSpec((None, s, ATT_HEAD_DIM), lambda g, i: (g, 0, 0)),
            pl.BlockSpec((s // tk, VT_ROWS, tk), lambda g, i: (0, g, 0)),
        ],
        out_specs=pl.BlockSpec((tq, ATT_REP * ATT_HEAD_DIM), lambda g, i: (i, g)),
        out_shape=jax.ShapeDtypeStruct((s, ATT_WIDTH), BF16),
        scratch_shapes=[pltpu.VMEM((SUBLANES, LANES), F32), pltpu.VMEM((1, nq), F32), pltpu.VMEM((VT_ROWS, nq), F32),
                        pltpu.VMEM((tk, nq), F32), pltpu.VMEM((tk, nq), F32)],
        compiler_params=_params("parallel", "arbitrary"),
        name="attention",
    )(qt, k, vt)


def _ssd_out_kernel(yf_ref, yb_ref, xs_ref, z_ref, dskip_ref, g_ref, w_ref, o_ref):
    y = yf_ref[...].astype(F32) + yb_ref[...].astype(F32) + xs_ref[...].astype(F32) * dskip_ref[...]
    y = y * _silu(z_ref[...].astype(F32))
    ms = jnp.mean(y * y, axis=-1, keepdims=True)
    yn = (y * lax.rsqrt(ms + EPS) * g_ref[...]).astype(BF16)
    o_ref[...] = jnp.dot(yn, w_ref[...], preferred_element_type=F32).astype(o_ref.dtype)


def _ssd_out(yf, yb, xbc, proj, dskip, g, w, tm):
    s = yf.shape[0]
    row = lambda i: (i, 0)
    const = lambda i: (0, 0)
    return pl.pallas_call(
        _ssd_out_kernel,
        grid=(s // tm,),
        in_specs=[
            pl.BlockSpec((tm, SSD_INNER), row), pl.BlockSpec((tm, SSD_INNER), row),
            pl.BlockSpec((tm, SSD_INNER), row), pl.BlockSpec((tm, SSD_INNER), row),
            pl.BlockSpec((1, SSD_INNER), const), pl.BlockSpec((1, SSD_INNER), const),
            pl.BlockSpec((SSD_INNER, D_MODEL), const),
        ],
        out_specs=pl.BlockSpec((tm, D_MODEL), row),
        out_shape=jax.ShapeDtypeStruct((s, D_MODEL), BF16),
        compiler_params=_params("parallel"),
        name="ssd_out",
    )(yf, yb, xbc, proj, dskip, g, w)


def _pool_kernel(um_ref, up_ref, un_ref, wp_ref, scale_ref, w_ref, o_ref, ext_ref, *, seq):
    tm = um_ref.shape[0]
    _fill_ext(ext_ref, um_ref, up_ref, un_ref, tm)
    t = pl.program_id(0) * tm + lax.broadcasted_iota(I32, (tm, 1), 0)
    mixed = []
    for gi, win in enumerate(POOL_WINDOWS):
        half = win // 2
        cols = slice(gi * POOL_GROUP_DIM, (gi + 1) * POOL_GROUP_DIM)
        acc = ext_ref[pl.ds(HALO - half, tm), cols]
        for k in range(1, win):
            acc = acc + ext_ref[pl.ds(HALO - half + k, tm), cols]
        cnt = (jnp.minimum(t + half, seq) - jnp.maximum(t - half, 0)).astype(F32)
        mean = acc / cnt
        mix = (mean - ext_ref[pl.ds(HALO, tm), cols]).astype(BF16)
        mixed.append(jnp.dot(mix, wp_ref[gi], preferred_element_type=F32))
    pooled = (jnp.concatenate(mixed, axis=1) * scale_ref[...]).astype(BF16)
    o_ref[...] = jnp.dot(pooled, w_ref[...], preferred_element_type=F32).astype(o_ref.dtype)


def _pool(proj, w_pool, scale, w_br, tm):
    s = proj.shape[0]
    return pl.pallas_call(
        functools.partial(_pool_kernel, seq=s),
        grid=(s // tm,),
        in_specs=_halo_specs(tm, POOL_WIDTH, COL_POOL // POOL_WIDTH, s) + [
            pl.BlockSpec((len(POOL_WINDOWS), POOL_GROUP_DIM, POOL_GROUP_DIM), lambda i: (0, 0, 0)),
            pl.BlockSpec((1, POOL_WIDTH), lambda i: (0, 0)),
            pl.BlockSpec((POOL_WIDTH, D_MODEL), lambda i: (0, 0)),
        ],
        out_specs=pl.BlockSpec((tm, D_MODEL), lambda i: (i, 0)),
        out_shape=jax.ShapeDtypeStruct((s, D_MODEL), BF16),
        scratch_shapes=[pltpu.VMEM((tm + 2 * HALO, POOL_WIDTH), F32)],
        compiler_params=_params("parallel"),
        name="pool",
    )(proj, proj, proj, w_pool, scale, w_br)


def _merge_kernel(bs_ref, att_ref, bp_ref, g0_ref, g1_ref, g2_ref, gb_ref, x_ref, watt_ref, wout_ref, o_ref):
    b_att = jnp.dot(att_ref[...], watt_ref[...], preferred_element_type=F32)
    gb = gb_ref[...]
    merged = (jax.nn.sigmoid(g0_ref[...] + gb[0:1]) * bs_ref[...]
              + jax.nn.sigmoid(g1_ref[...] + gb[1:2]) * b_att
              + jax.nn.sigmoid(g2_ref[...] + gb[2:3]) * bp_ref[...])
    o_ref[...] = x_ref[...] + jnp.dot(merged.astype(BF16), wout_ref[...], preferred_element_type=F32)


def _merge(b_ssd, att, b_pool, proj, gate_bias, x, w_att, w_out, tm):
    s = x.shape[0]
    row = lambda i: (i, 0)
    const = lambda i: (0, 0)
    gcol = COL_GATES // D_MODEL
    tile = pl.BlockSpec((tm, D_MODEL), row)
    return pl.pallas_call(
        _merge_kernel,
        grid=(s // tm,),
        in_specs=[
            tile, tile, tile,
            pl.BlockSpec((tm, D_MODEL), lambda i: (i, gcol)),
            pl.BlockSpec((tm, D_MODEL), lambda i: (i, gcol + 1)),
            pl.BlockSpec((tm, D_MODEL), lambda i: (i, gcol + 2)),
            pl.BlockSpec((N_BRANCH, D_MODEL), const),
            tile,
            pl.BlockSpec((ATT_WIDTH, D_MODEL), const),
            pl.BlockSpec((D_MODEL, D_MODEL), const),
        ],
        out_specs=tile,
        out_shape=jax.ShapeDtypeStruct((s, D_MODEL), F32),
        compiler_params=_params("parallel"),
        name="merge",
    )(b_ssd, att, b_pool, proj, proj, proj, gate_bias, x, w_att, w_out)


ROUTER_GROUP_ROW = 0
ROUTER_EXPERT_ROW = SUBLANES
TILE_TABLE_LANES = 256


def _first_argmax(v, idx, big):
    m = jnp.max(v, axis=0, keepdims=True)
    return m, jnp.min(jnp.where(v == m, idx, big), axis=0, keepdims=True)


def _router_kernel(x_ref, g_ref, wt_ref, b_ref, upper_ref, ltri_ref,
                   h_ref, eidx_ref, rank_ref, wts_ref, off_ref, tiles_ref, count_ref, *, tile_rows):
    i = pl.program_id(0)
    tm = x_ref.shape[0]

    @pl.when(i == 0)
    def _():
        count_ref[...] = jnp.zeros_like(count_ref)

    x = x_ref[...]
    ms = jnp.mean(x * x, axis=-1, keepdims=True)
    h = x * lax.rsqrt(ms + EPS) * g_ref[...]
    h_ref[...] = h
    lt = lax.dot_general(wt_ref[...], h, (((1,), (1,)), ((), ())), precision=HIGHEST,
                         preferred_element_type=F32) + b_ref[...]

    r8 = lax.broadcasted_iota(I32, (SUBLANES, tm), 0)
    gl = jnp.where(r8 < MOE_GROUPS, lt[ROUTER_GROUP_ROW:ROUTER_GROUP_ROW + SUBLANES], -jnp.inf)
    gmax, gsel = _first_argmax(gl, r8, SUBLANES)
    p_group = 1.0 / jnp.sum(jnp.exp(gl - gmax), axis=0, keepdims=True)

    r32 = lax.broadcasted_iota(I32, (N_EXPERTS, tm), 0)
    el = lt[ROUTER_EXPERT_ROW:ROUTER_EXPERT_ROW + N_EXPERTS]
    el = jnp.where(lax.shift_right_logical(r32, int(math.log2(MOE_PER_GROUP))) == gsel, el, -jnp.inf)
    m1, i1 = _first_argmax(el, r32, N_EXPERTS)
    el2 = jnp.where(r32 == i1, -jnp.inf, el)
    m2, i2 = _first_argmax(el2, r32, N_EXPERTS)
    z = jnp.sum(jnp.exp(el - m1), axis=0, keepdims=True)
    p1 = 1.0 / z
    p2 = jnp.exp(m2 - m1) / z
    psum = p1 + p2
    wts_ref[0:1, :] = p_group * (p1 / psum)
    wts_ref[1:2, :] = p_group * (p2 / psum)
    eidx_ref[0:1, :] = i1
    eidx_ref[1:2, :] = i2

    sel0 = r32 == i1
    sel1 = r32 == i2
    member = jnp.where(sel0 | sel1, 1.0, 0.0)
    before = jnp.dot(member.astype(BF16), upper_ref[...], preferred_element_type=F32) + count_ref[:, 0:1]
    rank_ref[0:1, :] = jnp.sum(jnp.where(sel0, before, 0.0), axis=0, keepdims=True).astype(I32)
    rank_ref[1:2, :] = jnp.sum(jnp.where(sel1, before, 0.0), axis=0, keepdims=True).astype(I32)
    count_ref[...] = count_ref[...] + jnp.sum(member, axis=1, keepdims=True)

    @pl.when(i == pl.num_programs(0) - 1)
    def _():
        cnt = count_ref[...]
        ntile = jnp.floor((cnt + (tile_rows - 1)) * (1.0 / tile_rows))
        start = jnp.dot(ltri_ref[...], ntile.astype(BF16), preferred_element_type=F32)
        off_ref[...] = (start * tile_rows).astype(I32)
        end = (start + ntile)[:, 0:1]
        tid = lax.broadcasted_iota(I32, (N_EXPERTS, TILE_TABLE_LANES), 1).astype(F32)
        owner = jnp.sum(jnp.where(end <= tid, 1.0, 0.0), axis=0, keepdims=True)
        owner = jnp.minimum(owner, N_EXPERTS - 1.0)
        nact = jnp.sum(ntile[:, 0:1], axis=0, keepdims=True)
        r = lax.broadcasted_iota(I32, (SUBLANES, TILE_TABLE_LANES), 0)
        tiles_ref[...] = jnp.where(r == 0, owner, nact).astype(I32)


def _router(x, g, wt, b, upper, ltri, tm, tile_rows):
    s = x.shape[0]
    row = lambda i: (i, 0)
    const = lambda i: (0, 0)
    lane = lambda i: (0, i)
    return pl.pallas_call(
        functools.partial(_router_kernel, tile_rows=tile_rows),
        grid=(s // tm,),
        in_specs=[
            pl.BlockSpec((tm, D_MODEL), row),
            pl.BlockSpec((1, D_MODEL), const),
            pl.BlockSpec((LANES, D_MODEL), const),
            pl.BlockSpec((LANES, 1), const),
            pl.BlockSpec((tm, tm), const),
            pl.BlockSpec((N_EXPERTS, N_EXPERTS), const),
        ],
        out_specs=[
            pl.BlockSpec((tm, D_MODEL), row),
            pl.BlockSpec((2, tm), lane),
            pl.BlockSpec((2, tm), lane),
            pl.BlockSpec((2, tm), lane),
            pl.BlockSpec((N_EXPERTS, LANES), const),
            pl.BlockSpec((SUBLANES, TILE_TABLE_LANES), const),
        ],
        out_shape=[
            jax.ShapeDtypeStruct((s, D_MODEL), F32),
            jax.ShapeDtypeStruct((2, s), I32),
            jax.ShapeDtypeStruct((2, s), I32),
            jax.ShapeDtypeStruct((2, s), F32),
            jax.ShapeDtypeStruct((N_EXPERTS, LANES), I32),
            jax.ShapeDtypeStruct((SUBLANES, TILE_TABLE_LANES), I32),
        ],
        scratch_shapes=[pltpu.VMEM((N_EXPERTS, LANES), F32)],
        compiler_params=_params("arbitrary"),
        name="router",
    )(x, g, wt, b, upper, ltri)


def _positions_kernel(eidx_ref, rank_ref, off_ref, pos_ref):
    e = eidx_ref[...]
    pos = rank_ref[...]
    for x in range(N_EXPERTS):
        pos = pos + jnp.where(e == x, off_ref[x:x + 1, 0:1], 0)
    pos_ref[...] = pos


def _positions(eidx, rank, off):
    full = lambda a: pl.BlockSpec(a.shape, lambda i: (0, 0))
    return pl.pallas_call(
        _positions_kernel,
        grid=(1,),
        in_specs=[full(eidx), full(rank), full(off)],
        out_specs=full(eidx),
        out_shape=jax.ShapeDtypeStruct(eidx.shape, I32),
        compiler_params=_params("arbitrary"),
        name="positions",
    )(eidx, rank, off)


ROW_DMA_UNROLL = 8


def _dispatch_kernel(pos_s, h_ref, xs_in_hbm, xs_hbm, sem, *, seq):
    del xs_in_hbm
    tile = h_ref.shape[0]
    base = pl.program_id(0) * tile

    def row_copy(t, slot):
        pos = pos_s[slot * seq + base + t]
        return pltpu.make_async_copy(h_ref.at[pl.ds(t, 1), :], xs_hbm.at[pl.ds(pos, 1), :], sem)

    def start(t, c):
        row_copy(t, 0).start()
        row_copy(t, 1).start()
        return c

    def wait(t, c):
        row_copy(t, 0).wait()
        row_copy(t, 1).wait()
        return c

    lax.fori_loop(0, tile, start, 0, unroll=ROW_DMA_UNROLL)
    lax.fori_loop(0, tile, wait, 0, unroll=ROW_DMA_UNROLL)


def _dispatch(pos, h, xs_zero, tile):
    s = h.shape[0]
    return pl.pallas_call(
        functools.partial(_dispatch_kernel, seq=s),
        grid_spec=pltpu.PrefetchScalarGridSpec(
            num_scalar_prefetch=1,
            grid=(s // tile,),
            in_specs=[pl.BlockSpec((tile, D_MODEL), lambda i, *_: (i, 0)), pl.BlockSpec(memory_space=pl.ANY)],
            out_specs=pl.BlockSpec(memory_space=pl.ANY),
            scratch_shapes=[pltpu.SemaphoreType.DMA],
        ),
        out_shape=jax.ShapeDtypeStruct(xs_zero.shape, xs_zero.dtype),
        input_output_aliases={2: 0},
        compiler_params=_params("arbitrary"),
        name="dispatch",
    )(pos, h, xs_zero)


def _ffn_kernel(owner_s, nact_s, x_ref, wg_ref, wu_ref, wd_ref, o_ref):
    j = pl.program_id(0)

    @pl.when(j < nact_s[0])
    def _():
        x = x_ref[...].astype(BF16)
        gate = jnp.dot(x, wg_ref[...], preferred_element_type=F32)
        up = jnp.dot(x, wu_ref[...], preferred_element_type=F32)
        hid = (_silu(gate) * up).astype(BF16)
        o_ref[...] = jnp.dot(hid, wd_ref[...], preferred_element_type=F32)

    @pl.when(j >= nact_s[0])
    def _():
        o_ref[...] = jnp.zeros_like(o_ref)


def _expert_ffn(owner, nact, xs, wg, wu, wd, tile_rows):
    rows = xs.shape[0]
    return pl.pallas_call(
        _ffn_kernel,
        grid_spec=pltpu.PrefetchScalarGridSpec(
            num_scalar_prefetch=2,
            grid=(rows // tile_rows,),
            in_specs=[
                pl.BlockSpec((tile_rows, D_MODEL), lambda j, o, n: (j, 0)),
                pl.BlockSpec((None, D_MODEL, EXPERT_FF), lambda j, o, n: (o[j], 0, 0)),
                pl.BlockSpec((None, D_MODEL, EXPERT_FF), lambda j, o, n: (o[j], 0, 0)),
                pl.BlockSpec((None, EXPERT_FF, D_MODEL), lambda j, o, n: (o[j], 0, 0)),
            ],
            out_specs=pl.BlockSpec((tile_rows, D_MODEL), lambda j, o, n: (j, 0)),
        ),
        out_shape=jax.ShapeDtypeStruct((rows, D_MODEL), F32),
        compiler_params=_params("arbitrary"),
        name="expert_ffn",
    )(owner, nact, xs, wg, wu, wd)


def _combine_kernel(pos_s, ys_hbm, x_ref, w_ref, o_ref, buf_ref, sem, *, seq):
    tile = x_ref.shape[0]
    base = pl.program_id(0) * tile

    def row_copy(t, slot):
        pos = pos_s[slot * seq + base + t]
        return pltpu.make_async_copy(ys_hbm.at[pl.ds(pos, 1), :], buf_ref.at[slot, pl.ds(t, 1), :], sem)

    def start(t, c):
        row_copy(t, 0).start()
        row_copy(t, 1).start()
        return c

    def wait(t, c):
        row_copy(t, 0).wait()
        row_copy(t, 1).wait()
        return c

    lax.fori_loop(0, tile, start, 0, unroll=ROW_DMA_UNROLL)
    lax.fori_loop(0, tile, wait, 0, unroll=ROW_DMA_UNROLL)
    w = w_ref[...]
    o_ref[...] = x_ref[...] + w[:, 0:1] * buf_ref[0] + w[:, 1:2] * buf_ref[1]


def _combine(pos, ys, x, w_cols, tile):
    s = x.shape[0]
    return pl.pallas_call(
        functools.partial(_combine_kernel, seq=s),
        grid_spec=pltpu.PrefetchScalarGridSpec(
            num_scalar_prefetch=1,
            grid=(s // tile,),
            in_specs=[
                pl.BlockSpec(memory_space=pl.ANY),
                pl.BlockSpec((tile, D_MODEL), lambda i, *_: (i, 0)),
                pl.BlockSpec((tile, 2), lambda i, *_: (i, 0)),
            ],
            out_specs=pl.BlockSpec((tile, D_MODEL), lambda i, *_: (i, 0)),
            scratch_shapes=[pltpu.VMEM((2, tile, D_MODEL), F32), pltpu.SemaphoreType.DMA],
        ),
        out_shape=jax.ShapeDtypeStruct((s, D_MODEL), F32),
        compiler_params=_params("arbitrary"),
        name="combine",
    )(pos, ys, x, w_cols)


def _tiles(s):
    return dict(
        inproj_tm=min(1024, s), inproj_tn=768,
        conv_tm=min(512, s), prep_tm=min(1024, s),
        attn_tq=256, attn_tk=256,
        tail_tm=min(256, s), pool_tm=min(512, s),
        router_tm=min(512, s), moe_rows=256, dispatch_tile=min(512, s), combine_tile=min(256, s),
    )


def _rope_tables(s):
    rows = s // GRID_W
    row = jnp.repeat(jnp.arange(rows), GRID_W).astype(F32)
    col = jnp.tile(jnp.arange(GRID_W), rows).astype(F32)
    half = ATT_HEAD_DIM // 2
    inv_freq = ROPE_THETA ** (-jnp.arange(0, half, 2, dtype=F32) / half)
    ang = jnp.concatenate([row[:, None] * inv_freq, col[:, None] * inv_freq], axis=-1)
    d = jnp.arange(LANES)
    idx = ((d % ATT_HEAD_DIM) // half) * (half // 2) + d % (half // 2)
    return jnp.cos(ang)[:, idx], jnp.sin(ang)[:, idx]


def _head_constants():
    d = jnp.arange(LANES)
    same_head = (d[:, None] // ATT_HEAD_DIM) == (d[None, :] // ATT_HEAD_DIM)
    bd = same_head.astype(BF16)
    quarter = ATT_HEAD_DIM // 4
    second = (d % (2 * quarter)) >= quarter
    src = jnp.where(second, d - quarter, d + quarter)
    sign = jnp.where(second, 1.0, -1.0)
    perm = jnp.zeros((LANES, LANES), F32).at[src, d].set(sign).astype(BF16)
    return bd, perm


def _expand_matrix(lane0):
    r = jnp.arange(LANES)[:, None]
    c = jnp.arange(SSD_INNER)[None, :]
    return (r == lane0 + c // SSD_HEAD_DIM).astype(BF16)


def _pad_lanes(v, width=LANES):
    flat = v.reshape(1, -1)
    return jnp.pad(flat, ((0, 0), (0, width - flat.shape[1])))


def _mixer_layer(x, t, cos, sin, bd, perm, e_f, e_b, norm_mix, w_in, conv_w, conv_b, dt_bias, a_log, d_skip,
                 ssd_norm, w_br_ssd, q_norm, k_norm, w_br_att, w_pool, pool_scale, w_br_pool, gate_bias, w_out):
    o_z, o_xbc = 0, SSD_INNER
    o_dt = o_xbc + SSD_CONV_DIM
    o_q = o_dt + 2 * SSD_HEADS
    o_kv = o_q + ATT_WIDTH
    o_pool = o_kv + 2 * ATT_KV_WIDTH
    o_gate = o_pool + POOL_WIDTH
    w_main = jnp.concatenate(
        [w_in[:, o_z:o_dt], w_in[:, o_gate:], w_in[:, o_q:o_kv], w_in[:, o_pool:o_gate], w_in[:, o_kv:o_pool]],
        axis=1).astype(BF16)
    w_dt = jnp.pad(w_in[:, o_dt:o_q], ((0, 0), (0, DT_PAD - 2 * SSD_HEADS))).astype(BF16)

    proj, dt_raw = _in_proj(x, norm_mix.reshape(1, -1), w_main, w_dt, t["inproj_tm"], t["inproj_tn"])

    xbc = _conv(proj, conv_w, conv_b.reshape(1, -1), t["conv_tm"])
    y_f, y_b = _ssd(xbc, dt_raw, _pad_lanes(dt_bias), _pad_lanes(a_log), e_f, e_b)
    b_ssd = _ssd_out(y_f, y_b, xbc, proj, jnp.repeat(d_skip, SSD_HEAD_DIM).reshape(1, -1),
                     ssd_norm.reshape(1, -1), w_br_ssd.astype(BF16), t["tail_tm"])

    qn = jnp.tile(q_norm, LANES // ATT_HEAD_DIM).reshape(1, -1)
    kn = jnp.tile(k_norm, LANES // ATT_HEAD_DIM).reshape(1, -1)
    qt = _q_prep(proj, cos, sin, qn, bd, perm, t["attn_tq"])
    k = _k_prep(proj, cos, sin, kn, bd, perm, t["prep_tm"])
    vt = _v_t(proj, t["attn_tk"])
    att = _attention(qt, k, vt, t["attn_tq"])

    b_pool = _pool(proj, w_pool.astype(BF16), pool_scale.reshape(1, -1), w_br_pool.astype(BF16), t["pool_tm"])
    return _merge(b_ssd, att, b_pool, proj, gate_bias, x, w_br_att.astype(BF16), w_out.astype(BF16), t["tail_tm"])


def _moe_layer(x, t, upper, ltri, norm_ffn, w_rg, b_rg, w_re, b_re, w_gate_e, w_up_e, w_down_e):
    s = x.shape[0]
    rows_per_tile = t["moe_rows"]
    wt = jnp.zeros((LANES, D_MODEL), F32)
    wt = wt.at[ROUTER_GROUP_ROW:ROUTER_GROUP_ROW + MOE_GROUPS].set(w_rg.T)
    wt = wt.at[ROUTER_EXPERT_ROW:ROUTER_EXPERT_ROW + N_EXPERTS].set(w_re.T)
    bias = jnp.zeros((LANES, 1), F32)
    bias = bias.at[ROUTER_GROUP_ROW:ROUTER_GROUP_ROW + MOE_GROUPS, 0].set(b_rg)
    bias = bias.at[ROUTER_EXPERT_ROW:ROUTER_EXPERT_ROW + N_EXPERTS, 0].set(b_re)

    h, eidx, rank, wts, off, tiles = _router(x, norm_ffn.reshape(1, -1), wt, bias, upper, ltri,
                                             t["router_tm"], rows_per_tile)
    pos = _positions(eidx, rank, off).reshape(-1)
    sorted_rows = 2 * s + N_EXPERTS * rows_per_tile
    xs = _dispatch(pos, h, jnp.zeros((sorted_rows, D_MODEL), F32), t["dispatch_tile"])
    ys = _expert_ffn(tiles[0], tiles[1, 0:1], xs, w_gate_e.astype(BF16), w_up_e.astype(BF16),
                     w_down_e.astype(BF16), rows_per_tile)
    return _combine(pos, ys, x, wts.T, t["combine_tile"])


def kernel(x, norm_mix, w_in, conv_w, conv_b, dt_bias, a_log, d_skip, ssd_norm, w_br_ssd, q_norm, k_norm, w_br_att, w_pool, pool_scale, w_br_pool, gate_bias, w_out, norm_ffn, w_router_group, b_router_group, w_router_expert, b_router_expert, w_gate_e, w_up_e, w_down_e):
    b, s, _ = x.shape
    assert b == 1 and s % 1024 == 0 and s % GRID_W == 0
    t = _tiles(s)
    assert 2 * s // t["moe_rows"] + N_EXPERTS <= TILE_TABLE_LANES
    cos, sin = _rope_tables(s)
    bd, perm = _head_constants()
    e_f, e_b = _expand_matrix(0), _expand_matrix(SSD_HEADS)
    rt = t["router_tm"]
    upper = (jnp.arange(rt)[:, None] < jnp.arange(rt)[None, :]).astype(BF16)
    ltri = (jnp.arange(N_EXPERTS)[None, :] < jnp.arange(N_EXPERTS)[:, None]).astype(BF16)

    xx = x[0]
    for l in range(DEPTH):
        xx = _mixer_layer(xx, t, cos, sin, bd, perm, e_f, e_b, norm_mix[l], w_in[l], conv_w[l], conv_b[l], dt_bias[l],
                          a_log[l], d_skip[l], ssd_norm[l], w_br_ssd[l], q_norm[l], k_norm[l], w_br_att[l],
                          w_pool[l], pool_scale[l], w_br_pool[l], gate_bias[l], w_out[l])
        xx = _moe_layer(xx, t, upper, ltri, norm_ffn[l], w_router_group[l], b_router_group[l],
                        w_router_expert[l], b_router_expert[l], w_gate_e[l], w_up_e[l], w_down_e[l])
    return xx[None]
```

```python
import functools
import math

import jax
import jax.numpy as jnp
from jax import lax
from jax.experimental import pallas as pl
from jax.experimental.pallas import tpu as pltpu

F32 = jnp.float32
BF16 = jnp.bfloat16
I32 = jnp.int32
HIGHEST = lax.Precision.HIGHEST

EPS = 1e-6
D_MODEL = 1024
DEPTH = 2

SSD_INNER = 2048
SSD_HEAD_DIM = 64
SSD_HEADS = 32
SSD_GROUPS = 4
SSD_STATE = 128
SSD_CONV = 5
SSD_CHUNK = 128
SSD_BC = SSD_GROUPS * SSD_STATE
SSD_CONV_DIM = SSD_INNER + 2 * SSD_BC
HEADS_PER_GROUP = SSD_HEADS // SSD_GROUPS
GROUP_WIDTH = HEADS_PER_GROUP * SSD_HEAD_DIM

ATT_HEADS = 16
ATT_KV_HEADS = 4
ATT_HEAD_DIM = 64
ATT_WIDTH = 1024
ATT_KV_WIDTH = 256
ATT_REP = ATT_HEADS // ATT_KV_HEADS
ROPE_THETA = 10000.0
GRID_W = 64

POOL_WIDTH = 1024
POOL_WINDOWS = (2, 4, 8, 16)
POOL_GROUP_DIM = 256

N_BRANCH = 3
MOE_GROUPS = 4
MOE_PER_GROUP = 8
N_EXPERTS = 32
EXPERT_FF = 512

LANES = 128
SUBLANES = 8
VMEM_LIMIT_BYTES = 56 * 1024 * 1024

COL_Z = 0
COL_XBC = COL_Z + SSD_INNER
COL_GATES = COL_XBC + SSD_CONV_DIM
COL_Q = COL_GATES + N_BRANCH * D_MODEL
COL_POOL = COL_Q + ATT_WIDTH
COL_K = COL_POOL + POOL_WIDTH
COL_V = COL_K + ATT_KV_WIDTH
PROJ_DIM = COL_V + ATT_KV_WIDTH
DT_PAD = LANES

HALO = 2 * SUBLANES


def _params(*sem):
    return pltpu.CompilerParams(dimension_semantics=sem, vmem_limit_bytes=VMEM_LIMIT_BYTES)


def _silu(v):
    return v * jax.nn.sigmoid(v)


def _split_bf16(v):
    hi = v.astype(BF16)
    lo = (v - hi.astype(F32)).astype(BF16)
    return hi, lo


def _inproj_kernel(x_ref, g_ref, w_ref, wdt_ref, o_ref, dt_ref, h_ref):
    @pl.when(pl.program_id(1) == 0)
    def _():
        x = x_ref[...]
        ms = jnp.mean(x * x, axis=-1, keepdims=True)
        h = (x * lax.rsqrt(ms + EPS) * g_ref[...]).astype(BF16)
        h_ref[...] = h
        dt_ref[...] = jnp.dot(h, wdt_ref[...], preferred_element_type=F32)

    o_ref[...] = jnp.dot(h_ref[...], w_ref[...], preferred_element_type=F32).astype(o_ref.dtype)


def _in_proj(x, g, w, wdt, tm, tn):
    s = x.shape[0]
    return pl.pallas_call(
        _inproj_kernel,
        grid=(s // tm, PROJ_DIM // tn),
        in_specs=[
            pl.BlockSpec((tm, D_MODEL), lambda i, j: (i, 0)),
            pl.BlockSpec((1, D_MODEL), lambda i, j: (0, 0)),
            pl.BlockSpec((D_MODEL, tn), lambda i, j: (0, j)),
            pl.BlockSpec((D_MODEL, DT_PAD), lambda i, j: (0, 0)),
        ],
        out_specs=[
            pl.BlockSpec((tm, tn), lambda i, j: (i, j)),
            pl.BlockSpec((tm, DT_PAD), lambda i, j: (i, 0)),
        ],
        out_shape=[jax.ShapeDtypeStruct((s, PROJ_DIM), BF16), jax.ShapeDtypeStruct((s, DT_PAD), F32)],
        scratch_shapes=[pltpu.VMEM((tm, D_MODEL), BF16)],
        compiler_params=_params("parallel", "arbitrary"),
        name="in_proj",
    )(x, g, w, wdt)


def _fill_ext(ext_ref, main_ref, prev_ref, next_ref, tm):
    i = pl.program_id(0)
    last = pl.num_programs(0) - 1
    ext_ref[0:HALO, :] = jnp.where(i > 0, prev_ref[...].astype(F32), 0.0)
    ext_ref[HALO:HALO + tm, :] = main_ref[...].astype(F32)
    ext_ref[HALO + tm:, :] = jnp.where(i < last, next_ref[...].astype(F32), 0.0)


def _conv_kernel(xm_ref, xp_ref, xn_ref, w_ref, b_ref, o_ref, ext_ref):
    tm = xm_ref.shape[0]
    _fill_ext(ext_ref, xm_ref, xp_ref, xn_ref, tm)
    pad = SSD_CONV // 2
    acc = jnp.broadcast_to(b_ref[...], o_ref.shape)
    for k in range(SSD_CONV):
        acc = acc + ext_ref[pl.ds(HALO - pad + k, tm), :] * w_ref[k:k + 1, :]
    o_ref[...] = _silu(acc).astype(o_ref.dtype)


def _halo_specs(tm, tc, col0, s):
    rb = tm // HALO
    nrb = s // HALO
    return [
        pl.BlockSpec((tm, tc), lambda i, j=0: (i, col0 + j)),
        pl.BlockSpec((HALO, tc), lambda i, j=0: (jnp.maximum(i * rb - 1, 0), col0 + j)),
        pl.BlockSpec((HALO, tc), lambda i, j=0: (jnp.minimum((i + 1) * rb, nrb - 1), col0 + j)),
    ]


def _conv(proj, w, b, tm):
    s = proj.shape[0]
    tc = 1024
    return pl.pallas_call(
        _conv_kernel,
        grid=(s // tm, SSD_CONV_DIM // tc),
        in_specs=_halo_specs(tm, tc, COL_XBC // tc, s) + [
            pl.BlockSpec((SSD_CONV, tc), lambda i, j: (0, j)),
            pl.BlockSpec((1, tc), lambda i, j: (0, j)),
        ],
        out_specs=pl.BlockSpec((tm, tc), lambda i, j: (i, j)),
        out_shape=jax.ShapeDtypeStruct((s, SSD_CONV_DIM), BF16),
        scratch_shapes=[pltpu.VMEM((tm + 2 * HALO, tc), F32)],
        compiler_params=_params("parallel", "parallel"),
        name="conv",
    )(proj, proj, proj, w, b)


def _ssd_direction(x_ref, b_ref, c_ref, dt_ref, e_ref, y_ref, st_ref, bias, a, fwd):
    q = SSD_CHUNK
    row = lax.broadcasted_iota(I32, (q, q), 0)
    col = lax.broadcasted_iota(I32, (q, q), 1)
    mask = (col <= row) if fwd else (col >= row)
    lane0 = 0 if fwd else SSD_HEADS

    dt = jax.nn.softplus(dt_ref[...] + bias)
    da = dt * a
    cs = jnp.dot(mask.astype(F32), da, precision=HIGHEST, preferred_element_type=F32)
    cs_t = cs.T
    edge = q - 1 if fwd else 0
    tot = cs[edge:edge + 1, :]
    dt_hi, dt_lo = _split_bf16(dt)
    dec = jnp.exp(tot)
    dec_hi = dec.astype(BF16).astype(F32)
    r16 = lax.broadcasted_iota(I32, (2 * SUBLANES, LANES), 0)
    dec_rows = jnp.where(r16 == 0, dec_hi, jnp.where(r16 == 1, dec - dec_hi, 0.0)).astype(BF16)
    lhs = jnp.concatenate(
        [dt_hi, dt_lo, jnp.exp(cs).astype(BF16), jnp.exp(tot - cs).astype(BF16), dec_rows], axis=0)
    ex = jnp.dot(lhs, e_ref[...], preferred_element_type=F32)
    dt_x = ex[0:q] + ex[q:2 * q]
    ecs_x = ex[2 * q:3 * q]
    dte_x = ex[3 * q:4 * q]
    dec_x = ex[4 * q:4 * q + 1] + ex[4 * q + 1:4 * q + 2]

    xdt = x_ref[...].astype(F32) * dt_x
    xdte_b = (xdt * dte_x).astype(BF16)
    lane = lax.broadcasted_iota(I32, (q, LANES), 1)
    first_half = lane < SSD_HEAD_DIM
    bm = b_ref[...].astype(BF16)
    cm = c_ref[...].astype(BF16)
    for g in range(SSD_GROUPS):
        bg = bm[:, g * SSD_STATE:(g + 1) * SSD_STATE]
        cg = cm[:, g * SSD_STATE:(g + 1) * SSD_STATE]
        cb = lax.dot_general(cg, bg, (((1,), (1,)), ((), ())), preferred_element_type=F32)
        gs = slice(g * GROUP_WIDTH, (g + 1) * GROUP_WIDTH)
        st = st_ref[g]
        y_off = jnp.dot(cg, st.astype(BF16), preferred_element_type=F32) * ecs_x[:, gs]
        for pr in range(HEADS_PER_GROUP // 2):
            c0 = g * GROUP_WIDTH + pr * LANES
            xp = xdt[:, c0:c0 + LANES]
            halves = (jnp.where(first_half, xp, 0.0).astype(BF16), jnp.where(first_half, 0.0, xp).astype(BF16))
            yd = None
            for k in range(2):
                hl = lane0 + g * HEADS_PER_GROUP + 2 * pr + k
                seg = cs[:, hl:hl + 1] - cs_t[hl:hl + 1, :]
                decay = jnp.exp(jnp.where(mask, seg, -jnp.inf))
                part = jnp.dot((cb * decay).astype(BF16), halves[k], preferred_element_type=F32)
                yd = part if yd is None else yd + part
            y_ref[:, c0:c0 + LANES] = (yd + y_off[:, pr * LANES:(pr + 1) * LANES]).astype(y_ref.dtype)
        upd = lax.dot_general(bg, xdte_b[:, gs], (((0,), (0,)), ((), ())), preferred_element_type=F32)
        st_ref[g] = st * dec_x[:, gs] + upd


def _ssd_kernel(xf_ref, bf_ref, cf_ref, dtf_ref, xb_ref, bb_ref, cb_ref, dtb_ref,
                bias_ref, alog_ref, ef_ref, eb_ref, yf_ref, yb_ref, stf_ref, stb_ref):
    @pl.when(pl.program_id(0) == 0)
    def _():
        stf_ref[...] = jnp.zeros_like(stf_ref)
        stb_ref[...] = jnp.zeros_like(stb_ref)

    bias = bias_ref[...]
    a = -jnp.exp(alog_ref[...])
    _ssd_direction(xf_ref, bf_ref, cf_ref, dtf_ref, ef_ref, yf_ref, stf_ref, bias, a, True)
    _ssd_direction(xb_ref, bb_ref, cb_ref, dtb_ref, eb_ref, yb_ref, stb_ref, bias, a, False)


def _ssd(xbc, dt_raw, dt_bias, a_log, e_f, e_b):
    s = xbc.shape[0]
    q = SSD_CHUNK
    nc = s // q
    nb = SSD_INNER // SSD_BC

    def chunk_specs(im):
        return [
            pl.BlockSpec((q, SSD_INNER), lambda i: (im(i), 0)),
            pl.BlockSpec((q, SSD_BC), lambda i: (im(i), nb)),
            pl.BlockSpec((q, SSD_BC), lambda i: (im(i), nb + 1)),
            pl.BlockSpec((q, DT_PAD), lambda i: (im(i), 0)),
        ]

    fwd = lambda i: i
    bwd = lambda i: nc - 1 - i
    const = lambda shape: pl.BlockSpec(shape, lambda i: (0,) * len(shape))
    st_shape = (SSD_GROUPS, SSD_STATE, GROUP_WIDTH)
    return pl.pallas_call(
        _ssd_kernel,
        grid=(nc,),
        in_specs=chunk_specs(fwd) + chunk_specs(bwd) + [
            const((1, DT_PAD)), const((1, DT_PAD)), const((LANES, SSD_INNER)), const((LANES, SSD_INNER))],
        out_specs=[pl.BlockSpec((q, SSD_INNER), lambda i: (fwd(i), 0)),
                   pl.BlockSpec((q, SSD_INNER), lambda i: (bwd(i), 0))],
        out_shape=[jax.ShapeDtypeStruct((s, SSD_INNER), BF16)] * 2,
        scratch_shapes=[pltpu.VMEM(st_shape, F32), pltpu.VMEM(st_shape, F32)],
        compiler_params=_params("arbitrary"),
        name="ssd",
    )(xbc, xbc, xbc, dt_raw, xbc, xbc, xbc, dt_raw, dt_bias, a_log, e_f, e_b)


def _norm_rope(x, cos, sin, g, bd, perm):
    hi, lo = _split_bf16(x * x)
    ss = jnp.dot(hi, bd, preferred_element_type=F32) + jnp.dot(lo, bd, preferred_element_type=F32)
    y = x * lax.rsqrt(ss * (1.0 / ATT_HEAD_DIM) + EPS) * g
    yh, yl = _split_bf16(y)
    rot = jnp.dot(yh, perm, preferred_element_type=F32) + jnp.dot(yl, perm, preferred_element_type=F32)
    return y * cos + rot * sin


def _qprep_kernel(x_ref, cos_ref, sin_ref, g_ref, bd_ref, perm_ref, o_ref, *, scale):
    tq = x_ref.shape[0]
    cos, sin, g, bd, perm = cos_ref[...], sin_ref[...], g_ref[...], bd_ref[...], perm_ref[...]
    for j in range(ATT_WIDTH // LANES):
        out = _norm_rope(x_ref[:, j * LANES:(j + 1) * LANES].astype(F32), cos, sin, g, bd, perm) * scale
        out_t = out.T.astype(BF16)
        grp, r0 = (2 * j) // ATT_REP, (2 * j) % ATT_REP
        o_ref[grp, :, r0 * tq:(r0 + 1) * tq] = out_t[:ATT_HEAD_DIM]
        o_ref[grp, :, (r0 + 1) * tq:(r0 + 2) * tq] = out_t[ATT_HEAD_DIM:]


def _kprep_kernel(x_ref, cos_ref, sin_ref, g_ref, bd_ref, perm_ref, o_ref):
    out = _norm_rope(x_ref[...].astype(F32), cos_ref[...], sin_ref[...], g_ref[...], bd_ref[...],
                     perm_ref[...]).astype(BF16)
    o_ref[0] = out[:, :ATT_HEAD_DIM]
    o_ref[1] = out[:, ATT_HEAD_DIM:]


VT_ROWS = ATT_HEAD_DIM + 2 * SUBLANES


def _vt_kernel(x_ref, o_ref):
    xt = x_ref[...].astype(F32).T.astype(BF16)
    pad = VT_ROWS - ATT_HEAD_DIM
    ones_row = (lax.broadcasted_iota(I32, (pad, xt.shape[1]), 0) == 0).astype(BF16)
    for g in range(ATT_KV_HEADS):
        o_ref[g * VT_ROWS:g * VT_ROWS + ATT_HEAD_DIM, :] = xt[g * ATT_HEAD_DIM:(g + 1) * ATT_HEAD_DIM]
        o_ref[g * VT_ROWS + ATT_HEAD_DIM:(g + 1) * VT_ROWS, :] = ones_row


def _rope_specs(tm):
    return [
        pl.BlockSpec((tm, LANES), lambda i, j: (i, 0)),
        pl.BlockSpec((tm, LANES), lambda i, j: (i, 0)),
        pl.BlockSpec((1, LANES), lambda i, j: (0, 0)),
        pl.BlockSpec((LANES, LANES), lambda i, j: (0, 0)),
        pl.BlockSpec((LANES, LANES), lambda i, j: (0, 0)),
    ]


def _q_prep(proj, cos, sin, g, bd, perm, tq):
    s = proj.shape[0]
    scale = ATT_HEAD_DIM ** -0.5 * math.log2(math.e)
    const = lambda i: (0, 0)
    return pl.pallas_call(
        functools.partial(_qprep_kernel, scale=scale),
        grid=(s // tq,),
        in_specs=[
            pl.BlockSpec((tq, ATT_WIDTH), lambda i: (i, COL_Q // ATT_WIDTH)),
            pl.BlockSpec((tq, LANES), lambda i: (i, 0)),
            pl.BlockSpec((tq, LANES), lambda i: (i, 0)),
            pl.BlockSpec((1, LANES), const),
            pl.BlockSpec((LANES, LANES), const),
            pl.BlockSpec((LANES, LANES), const),
        ],
        out_specs=pl.BlockSpec((ATT_KV_HEADS, ATT_HEAD_DIM, ATT_REP * tq), lambda i: (0, 0, i)),
        out_shape=jax.ShapeDtypeStruct((ATT_KV_HEADS, ATT_HEAD_DIM, ATT_REP * s), BF16),
        compiler_params=_params("parallel"),
        name="q_prep",
    )(proj, cos, sin, g, bd, perm)


def _k_prep(proj, cos, sin, g, bd, perm, tm):
    s = proj.shape[0]
    return pl.pallas_call(
        _kprep_kernel,
        grid=(s // tm, ATT_KV_WIDTH // LANES),
        in_specs=[pl.BlockSpec((tm, LANES), lambda i, j: (i, COL_K // LANES + j))] + _rope_specs(tm),
        out_specs=pl.BlockSpec((2, tm, ATT_HEAD_DIM), lambda i, j: (j, i, 0)),
        out_shape=jax.ShapeDtypeStruct((ATT_KV_HEADS, s, ATT_HEAD_DIM), BF16),
        compiler_params=_params("parallel", "parallel"),
        name="k_prep",
    )(proj, cos, sin, g, bd, perm)


def _v_t(proj, tk):
    s = proj.shape[0]
    rows = ATT_KV_HEADS * VT_ROWS
    return pl.pallas_call(
        _vt_kernel,
        grid=(s // tk,),
        in_specs=[pl.BlockSpec((tk, ATT_KV_WIDTH), lambda i: (i, COL_V // ATT_KV_WIDTH))],
        out_specs=pl.BlockSpec((None, rows, tk), lambda i: (i, 0, 0)),
        out_shape=jax.ShapeDtypeStruct((s // tk, rows, tk), BF16),
        compiler_params=_params("parallel"),
        name="v_t",
    )(proj)


SAFE_SCORE_SPAN = 100.0


FAST_SWEEP_UNROLL = 8


def _attn_kernel(qt_ref, k_ref, vt_ref, o_ref, kmax_ref, shift_ref, acc_ref, pa_ref, pb_ref, sa_ref, sb_ref, *, tk):
    s = k_ref.shape[0]
    nq = qt_ref.shape[1]
    n_chunks = s // tk

    @pl.when(pl.program_id(1) == 0)
    def _():
        def key_norm(c, best):
            kc = k_ref[pl.ds(pl.multiple_of(c * tk, tk), tk), :].astype(F32)
            norm = jnp.sum(kc * kc, axis=1, keepdims=True)
            return jnp.maximum(best, jnp.max(norm, axis=0, keepdims=True))

        best = lax.fori_loop(0, n_chunks, key_norm, jnp.zeros((1, 1), F32))
        kmax_ref[...] = jnp.broadcast_to(best, kmax_ref.shape)

    q = qt_ref[...].astype(F32)
    bound = jnp.sqrt(jnp.sum(q * q, axis=0, keepdims=True) * kmax_ref[0:1, 0:1])
    safe = 2.0 * jnp.max(bound) <= SAFE_SCORE_SPAN
    acc_ref[...] = jnp.zeros_like(acc_ref)

    def scores(c):
        kc = k_ref[pl.ds(pl.multiple_of(c * tk, tk), tk), :]
        return jnp.dot(kc, qt_ref[...], preferred_element_type=F32)

    def sweep(produce, consume, buf_a, buf_b, unroll):
        produce(0, buf_a)

        def body(i, carry):
            c = 2 * i
            produce(c + 1, buf_b)
            consume(c, buf_a)
            produce(jnp.minimum(c + 2, n_chunks - 1), buf_a)
            consume(c + 1, buf_b)
            return carry

        lax.fori_loop(0, n_chunks // 2, body, 0, unroll=unroll)

    @pl.when(safe)
    def _():
        def produce(c, dst_ref):
            dst_ref[...] = jnp.exp2(scores(c) - bound).astype(BF16)

        def consume(c, src_ref):
            acc_ref[...] += jnp.dot(vt_ref[c], src_ref[...], preferred_element_type=F32)

        sweep(produce, consume, pa_ref, pb_ref, unroll=FAST_SWEEP_UNROLL)

    @pl.when(jnp.logical_not(safe))
    def _():
        shift_ref[...] = jnp.full_like(shift_ref, -jnp.inf)

        def produce(c, dst_ref):
            dst_ref[...] = scores(c)

        def consume(c, src_ref):
            st = src_ref[...]
            m_old = shift_ref[...]
            m_new = jnp.maximum(m_old, jnp.max(st, axis=0, keepdims=True))
            p = jnp.exp2(st - m_new).astype(BF16)
            acc_ref[...] = jnp.exp2(m_old - m_new) * acc_ref[...] + jnp.dot(vt_ref[c], p, preferred_element_type=F32)
            shift_ref[...] = m_new

        sweep(produce, consume, sa_ref, sb_ref, unroll=1)

    acc = acc_ref[...]
    out = acc[:ATT_HEAD_DIM] / acc[ATT_HEAD_DIM:ATT_HEAD_DIM + 1]
    tq = nq // ATT_REP
    heads = [out[:, r * tq:(r + 1) * tq] for r in range(ATT_REP)]
    o_ref[...] = jnp.concatenate(heads, axis=0).T.astype(o_ref.dtype)


def _attention(qt, k, vt, tq):
    s = k.shape[1]
    tk = vt.shape[2]
    nq = ATT_REP * tq
    return pl.pallas_call(
        functools.partial(_attn_kernel, tk=tk),
        grid=(ATT_KV_HEADS, s // tq),
        in_specs=[
            pl.BlockSpec((None, ATT_HEAD_DIM, nq), lambda g, i: (g, 0, i)),
            pl.BlockSpec((None, s, ATT_HEAD_DIM), lambda g, i: (g, 0, 0)),
            pl.BlockSpec((s // tk, VT_ROWS, tk), lambda g, i: (0, g, 0)),
        ],
        out_specs=pl.BlockSpec((tq, ATT_REP * ATT_HEAD_DIM), lambda g, i: (i, g)),
        out_shape=jax.ShapeDtypeStruct((s, ATT_WIDTH), BF16),
        scratch_shapes=[pltpu.VMEM((SUBLANES, LANES), F32), pltpu.VMEM((1, nq), F32), pltpu.VMEM((VT_ROWS, nq), F32),
                        pltpu.VMEM((tk, nq), BF16), pltpu.VMEM((tk, nq), BF16),
                        pltpu.VMEM((tk, nq), F32), pltpu.VMEM((tk, nq), F32)],
        compiler_params=_params("parallel", "arbitrary"),
        name="attention",
    )(qt, k, vt)


def _ssd_out_kernel(yf_ref, yb_ref, xs_ref, z_ref, dskip_ref, g_ref, w_ref, o_ref):
    y = yf_ref[...].astype(F32) + yb_ref[...].astype(F32) + xs_ref[...].astype(F32) * dskip_ref[...]
    y = y * _silu(z_ref[...].astype(F32))
    ms = jnp.mean(y * y, axis=-1, keepdims=True)
    yn = (y * lax.rsqrt(ms + EPS) * g_ref[...]).astype(BF16)
    o_ref[...] = jnp.dot(yn, w_ref[...], preferred_element_type=F32).astype(o_ref.dtype)


def _ssd_out(yf, yb, xbc, proj, dskip, g, w, tm):
    s = yf.shape[0]
    row = lambda i: (i, 0)
    const = lambda i: (0, 0)
    return pl.pallas_call(
        _ssd_out_kernel,
        grid=(s // tm,),
        in_specs=[
            pl.BlockSpec((tm, SSD_INNER), row), pl.BlockSpec((tm, SSD_INNER), row),
            pl.BlockSpec((tm, SSD_INNER), row), pl.BlockSpec((tm, SSD_INNER), row),
            pl.BlockSpec((1, SSD_INNER), const), pl.BlockSpec((1, SSD_INNER), const),
            pl.BlockSpec((SSD_INNER, D_MODEL), const),
        ],
        out_specs=pl.BlockSpec((tm, D_MODEL), row),
        out_shape=jax.ShapeDtypeStruct((s, D_MODEL), BF16),
        compiler_params=_params("parallel"),
        name="ssd_out",
    )(yf, yb, xbc, proj, dskip, g, w)


def _pool_kernel(um_ref, up_ref, un_ref, wp_ref, scale_ref, w_ref, o_ref, ext_ref, *, seq):
    tm = um_ref.shape[0]
    _fill_ext(ext_ref, um_ref, up_ref, un_ref, tm)
    t = pl.program_id(0) * tm + lax.broadcasted_iota(I32, (tm, 1), 0)
    mixed = []
    for gi, win in enumerate(POOL_WINDOWS):
        half = win // 2
        cols = slice(gi * POOL_GROUP_DIM, (gi + 1) * POOL_GROUP_DIM)
        acc = ext_ref[pl.ds(HALO - half, tm), cols]
        for k in range(1, win):
            acc = acc + ext_ref[pl.ds(HALO - half + k, tm), cols]
        cnt = (jnp.minimum(t + half, seq) - jnp.maximum(t - half, 0)).astype(F32)
        mean = acc / cnt
        mix = (mean - ext_ref[pl.ds(HALO, tm), cols]).astype(BF16)
        mixed.append(jnp.dot(mix, wp_ref[gi], preferred_element_type=F32))
    pooled = (jnp.concatenate(mixed, axis=1) * scale_ref[...]).astype(BF16)
    o_ref[...] = jnp.dot(pooled, w_ref[...], preferred_element_type=F32).astype(o_ref.dtype)


def _pool(proj, w_pool, scale, w_br, tm):
    s = proj.shape[0]
    return pl.pallas_call(
        functools.partial(_pool_kernel, seq=s),
        grid=(s // tm,),
        in_specs=_halo_specs(tm, POOL_WIDTH, COL_POOL // POOL_WIDTH, s) + [
            pl.BlockSpec((len(POOL_WINDOWS), POOL_GROUP_DIM, POOL_GROUP_DIM), lambda i: (0, 0, 0)),
            pl.BlockSpec((1, POOL_WIDTH), lambda i: (0, 0)),
            pl.BlockSpec((POOL_WIDTH, D_MODEL), lambda i: (0, 0)),
        ],
        out_specs=pl.BlockSpec((tm, D_MODEL), lambda i: (i, 0)),
        out_shape=jax.ShapeDtypeStruct((s, D_MODEL), BF16),
        scratch_shapes=[pltpu.VMEM((tm + 2 * HALO, POOL_WIDTH), F32)],
        compiler_params=_params("parallel"),
        name="pool",
    )(proj, proj, proj, w_pool, scale, w_br)


def _merge_kernel(bs_ref, att_ref, bp_ref, g0_ref, g1_ref, g2_ref, gb_ref, x_ref, watt_ref, wout_ref, o_ref):
    b_att = jnp.dot(att_ref[...], watt_ref[...], preferred_element_type=F32)
    gb = gb_ref[...]
    merged = (jax.nn.sigmoid(g0_ref[...] + gb[0:1]) * bs_ref[...]
              + jax.nn.sigmoid(g1_ref[...] + gb[1:2]) * b_att
              + jax.nn.sigmoid(g2_ref[...] + gb[2:3]) * bp_ref[...])
    o_ref[...] = x_ref[...] + jnp.dot(merged.astype(BF16), wout_ref[...], preferred_element_type=F32)


def _merge(b_ssd, att, b_pool, proj, gate_bias, x, w_att, w_out, tm):
    s = x.shape[0]
    row = lambda i: (i, 0)
    const = lambda i: (0, 0)
    gcol = COL_GATES // D_MODEL
    tile = pl.BlockSpec((tm, D_MODEL), row)
    return pl.pallas_call(
        _merge_kernel,
        grid=(s // tm,),
        in_specs=[
            tile, tile, tile,
            pl.BlockSpec((tm, D_MODEL), lambda i: (i, gcol)),
            pl.BlockSpec((tm, D_MODEL), lambda i: (i, gcol + 1)),
            pl.BlockSpec((tm, D_MODEL), lambda i: (i, gcol + 2)),
            pl.BlockSpec((N_BRANCH, D_MODEL), const),
            tile,
            pl.BlockSpec((ATT_WIDTH, D_MODEL), const),
            pl.BlockSpec((D_MODEL, D_MODEL), const),
        ],
        out_specs=tile,
        out_shape=jax.ShapeDtypeStruct((s, D_MODEL), F32),
        compiler_params=_params("parallel"),
        name="merge",
    )(b_ssd, att, b_pool, proj, proj, proj, gate_bias, x, w_att, w_out)


ROUTER_GROUP_ROW = 0
ROUTER_EXPERT_ROW = SUBLANES
TILE_TABLE_LANES = 256


def _first_argmax(v, idx, big):
    m = jnp.max(v, axis=0, keepdims=True)
    return m, jnp.min(jnp.where(v == m, idx, big), axis=0, keepdims=True)


def _router_kernel(x_ref, g_ref, wt_ref, b_ref, upper_ref, ltri_ref,
                   h_ref, eidx_ref, rank_ref, wts_ref, off_ref, tiles_ref, count_ref, *, tile_rows):
    i = pl.program_id(0)
    tm = x_ref.shape[0]

    @pl.when(i == 0)
    def _():
        count_ref[...] = jnp.zeros_like(count_ref)

    x = x_ref[...]
    ms = jnp.mean(x * x, axis=-1, keepdims=True)
    h = x * lax.rsqrt(ms + EPS) * g_ref[...]
    h_ref[...] = h
    lt = lax.dot_general(wt_ref[...], h, (((1,), (1,)), ((), ())), precision=HIGHEST,
                         preferred_element_type=F32) + b_ref[...]

    r8 = lax.broadcasted_iota(I32, (SUBLANES, tm), 0)
    gl = jnp.where(r8 < MOE_GROUPS, lt[ROUTER_GROUP_ROW:ROUTER_GROUP_ROW + SUBLANES], -jnp.inf)
    gmax, gsel = _first_argmax(gl, r8, SUBLANES)
    p_group = 1.0 / jnp.sum(jnp.exp(gl - gmax), axis=0, keepdims=True)

    r32 = lax.broadcasted_iota(I32, (N_EXPERTS, tm), 0)
    el = lt[ROUTER_EXPERT_ROW:ROUTER_EXPERT_ROW + N_EXPERTS]
    el = jnp.where(lax.shift_right_logical(r32, int(math.log2(MOE_PER_GROUP))) == gsel, el, -jnp.inf)
    m1, i1 = _first_argmax(el, r32, N_EXPERTS)
    el2 = jnp.where(r32 == i1, -jnp.inf, el)
    m2, i2 = _first_argmax(el2, r32, N_EXPERTS)
    z = jnp.sum(jnp.exp(el - m1), axis=0, keepdims=True)
    p1 = 1.0 / z
    p2 = jnp.exp(m2 - m1) / z
    psum = p1 + p2
    wts_ref[0:1, :] = p_group * (p1 / psum)
    wts_ref[1:2, :] = p_group * (p2 / psum)
    eidx_ref[0:1, :] = i1
    eidx_ref[1:2, :] = i2

    sel0 = r32 == i1
    sel1 = r32 == i2
    member = jnp.where(sel0 | sel1, 1.0, 0.0)
    before = jnp.dot(member.astype(BF16), upper_ref[...], preferred_element_type=F32) + count_ref[:, 0:1]
    rank_ref[0:1, :] = jnp.sum(jnp.where(sel0, before, 0.0), axis=0, keepdims=True).astype(I32)
    rank_ref[1:2, :] = jnp.sum(jnp.where(sel1, before, 0.0), axis=0, keepdims=True).astype(I32)
    count_ref[...] = count_ref[...] + jnp.sum(member, axis=1, keepdims=True)

    @pl.when(i == pl.num_programs(0) - 1)
    def _():
        cnt = count_ref[...]
        ntile = jnp.floor((cnt + (tile_rows - 1)) * (1.0 / tile_rows))
        start = jnp.dot(ltri_ref[...], ntile.astype(BF16), preferred_element_type=F32)
        off_ref[...] = (start * tile_rows).astype(I32)
        end = (start + ntile)[:, 0:1]
        tid = lax.broadcasted_iota(I32, (N_EXPERTS, TILE_TABLE_LANES), 1).astype(F32)
        owner = jnp.sum(jnp.where(end <= tid, 1.0, 0.0), axis=0, keepdims=True)
        owner = jnp.minimum(owner, N_EXPERTS - 1.0)
        nact = jnp.sum(ntile[:, 0:1], axis=0, keepdims=True)
        r = lax.broadcasted_iota(I32, (SUBLANES, TILE_TABLE_LANES), 0)
        tiles_ref[...] = jnp.where(r == 0, owner, nact).astype(I32)


def _router(x, g, wt, b, upper, ltri, tm, tile_rows):
    s = x.shape[0]
    row = lambda i: (i, 0)
    const = lambda i: (0, 0)
    lane = lambda i: (0, i)
    return pl.pallas_call(
        functools.partial(_router_kernel, tile_rows=tile_rows),
        grid=(s // tm,),
        in_specs=[
            pl.BlockSpec((tm, D_MODEL), row),
            pl.BlockSpec((1, D_MODEL), const),
            pl.BlockSpec((LANES, D_MODEL), const),
            pl.BlockSpec((LANES, 1), const),
            pl.BlockSpec((tm, tm), const),
            pl.BlockSpec((N_EXPERTS, N_EXPERTS), const),
        ],
        out_specs=[
            pl.BlockSpec((tm, D_MODEL), row),
            pl.BlockSpec((2, tm), lane),
            pl.BlockSpec((2, tm), lane),
            pl.BlockSpec((2, tm), lane),
            pl.BlockSpec((N_EXPERTS, LANES), const),
            pl.BlockSpec((SUBLANES, TILE_TABLE_LANES), const),
        ],
        out_shape=[
            jax.ShapeDtypeStruct((s, D_MODEL), F32),
            jax.ShapeDtypeStruct((2, s), I32),
            jax.ShapeDtypeStruct((2, s), I32),
            jax.ShapeDtypeStruct((2, s), F32),
            jax.ShapeDtypeStruct((N_EXPERTS, LANES), I32),
            jax.ShapeDtypeStruct((SUBLANES, TILE_TABLE_LANES), I32),
        ],
        scratch_shapes=[pltpu.VMEM((N_EXPERTS, LANES), F32)],
        compiler_params=_params("arbitrary"),
        name="router",
    )(x, g, wt, b, upper, ltri)


def _positions_kernel(eidx_ref, rank_ref, off_ref, pos_ref):
    e = eidx_ref[...]
    pos = rank_ref[...]
    for x in range(N_EXPERTS):
        pos = pos + jnp.where(e == x, off_ref[x:x + 1, 0:1], 0)
    pos_ref[...] = pos


def _positions(eidx, rank, off):
    full = lambda a: pl.BlockSpec(a.shape, lambda i: (0, 0))
    return pl.pallas_call(
        _positions_kernel,
        grid=(1,),
        in_specs=[full(eidx), full(rank), full(off)],
        out_specs=full(eidx),
        out_shape=jax.ShapeDtypeStruct(eidx.shape, I32),
        compiler_params=_params("arbitrary"),
        name="positions",
    )(eidx, rank, off)


ROW_DMA_UNROLL = 8


def _dispatch_kernel(pos_s, h_ref, xs_in_hbm, xs_hbm, sem, *, seq):
    del xs_in_hbm
    tile = h_ref.shape[0]
    base = pl.program_id(0) * tile

    def row_copy(t, slot):
        pos = pos_s[slot * seq + base + t]
        return pltpu.make_async_copy(h_ref.at[pl.ds(t, 1), :], xs_hbm.at[pl.ds(pos, 1), :], sem)

    def start(t, c):
        row_copy(t, 0).start()
        row_copy(t, 1).start()
        return c

    def wait(t, c):
        row_copy(t, 0).wait()
        row_copy(t, 1).wait()
        return c

    lax.fori_loop(0, tile, start, 0, unroll=ROW_DMA_UNROLL)
    lax.fori_loop(0, tile, wait, 0, unroll=ROW_DMA_UNROLL)


def _dispatch(pos, h, xs_zero, tile):
    s = h.shape[0]
    return pl.pallas_call(
        functools.partial(_dispatch_kernel, seq=s),
        grid_spec=pltpu.PrefetchScalarGridSpec(
            num_scalar_prefetch=1,
            grid=(s // tile,),
            in_specs=[pl.BlockSpec((tile, D_MODEL), lambda i, *_: (i, 0)), pl.BlockSpec(memory_space=pl.ANY)],
            out_specs=pl.BlockSpec(memory_space=pl.ANY),
            scratch_shapes=[pltpu.SemaphoreType.DMA],
        ),
        out_shape=jax.ShapeDtypeStruct(xs_zero.shape, xs_zero.dtype),
        input_output_aliases={2: 0},
        compiler_params=_params("arbitrary"),
        name="dispatch",
    )(pos, h, xs_zero)


def _ffn_kernel(owner_s, nact_s, x_ref, wg_ref, wu_ref, wd_ref, o_ref):
    j = pl.program_id(0)

    @pl.when(j < nact_s[0])
    def _():
        x = x_ref[...].astype(BF16)
        gate = jnp.dot(x, wg_ref[...].astype(BF16), preferred_element_type=F32)
        up = jnp.dot(x, wu_ref[...].astype(BF16), preferred_element_type=F32)
        hid = (_silu(gate) * up).astype(BF16)
        o_ref[...] = jnp.dot(hid, wd_ref[...].astype(BF16), preferred_element_type=F32)

    @pl.when(j >= nact_s[0])
    def _():
        o_ref[...] = jnp.zeros_like(o_ref)


def _expert_ffn(owner, nact, xs, wg, wu, wd, tile_rows):
    rows = xs.shape[0]
    return pl.pallas_call(
        _ffn_kernel,
        grid_spec=pltpu.PrefetchScalarGridSpec(
            num_scalar_prefetch=2,
            grid=(rows // tile_rows,),
            in_specs=[
                pl.BlockSpec((tile_rows, D_MODEL), lambda j, o, n: (j, 0)),
                pl.BlockSpec((None, D_MODEL, EXPERT_FF), lambda j, o, n: (o[j], 0, 0)),
                pl.BlockSpec((None, D_MODEL, EXPERT_FF), lambda j, o, n: (o[j], 0, 0)),
                pl.BlockSpec((None, EXPERT_FF, D_MODEL), lambda j, o, n: (o[j], 0, 0)),
            ],
            out_specs=pl.BlockSpec((tile_rows, D_MODEL), lambda j, o, n: (j, 0)),
        ),
        out_shape=jax.ShapeDtypeStruct((rows, D_MODEL), F32),
        compiler_params=_params("arbitrary"),
        name="expert_ffn",
    )(owner, nact, xs, wg, wu, wd)


def _combine_kernel(pos_s, ys_hbm, x_ref, w_ref, o_ref, buf_ref, sem, *, seq):
    tile = x_ref.shape[0]
    base = pl.program_id(0) * tile

    def row_copy(t, slot):
        pos = pos_s[slot * seq + base + t]
        return pltpu.make_async_copy(ys_hbm.at[pl.ds(pos, 1), :], buf_ref.at[slot, pl.ds(t, 1), :], sem)

    def start(t, c):
        row_copy(t, 0).start()
        row_copy(t, 1).start()
        return c

    def wait(t, c):
        row_copy(t, 0).wait()
        row_copy(t, 1).wait()
        return c

    lax.fori_loop(0, tile, start, 0, unroll=ROW_DMA_UNROLL)
    lax.fori_loop(0, tile, wait, 0, unroll=ROW_DMA_UNROLL)
    w = w_ref[...]
    o_ref[...] = x_ref[...] + w[:, 0:1] * buf_ref[0] + w[:, 1:2] * buf_ref[1]


def _combine(pos, ys, x, w_cols, tile):
    s = x.shape[0]
    return pl.pallas_call(
        functools.partial(_combine_kernel, seq=s),
        grid_spec=pltpu.PrefetchScalarGridSpec(
            num_scalar_prefetch=1,
            grid=(s // tile,),
            in_specs=[
                pl.BlockSpec(memory_space=pl.ANY),
                pl.BlockSpec((tile, D_MODEL), lambda i, *_: (i, 0)),
                pl.BlockSpec((tile, 2), lambda i, *_: (i, 0)),
            ],
            out_specs=pl.BlockSpec((tile, D_MODEL), lambda i, *_: (i, 0)),
            scratch_shapes=[pltpu.VMEM((2, tile, D_MODEL), F32), pltpu.SemaphoreType.DMA],
        ),
        out_shape=jax.ShapeDtypeStruct((s, D_MODEL), F32),
        compiler_params=_params("arbitrary"),
        name="combine",
    )(pos, ys, x, w_cols)


def _tiles(s):
    return dict(
        inproj_tm=min(1024, s), inproj_tn=768,
        conv_tm=min(512, s), prep_tm=min(1024, s),
        attn_tq=256, attn_tk=256,
        tail_tm=min(256, s), pool_tm=min(512, s),
        router_tm=min(512, s), moe_rows=256, dispatch_tile=min(512, s), combine_tile=min(256, s),
    )


def _rope_tables(s):
    rows = s // GRID_W
    row = jnp.repeat(jnp.arange(rows), GRID_W).astype(F32)
    col = jnp.tile(jnp.arange(GRID_W), rows).astype(F32)
    half = ATT_HEAD_DIM // 2
    inv_freq = ROPE_THETA ** (-jnp.arange(0, half, 2, dtype=F32) / half)
    ang = jnp.concatenate([row[:, None] * inv_freq, col[:, None] * inv_freq], axis=-1)
    d = jnp.arange(LANES)
    idx = ((d % ATT_HEAD_DIM) // half) * (half // 2) + d % (half // 2)
    return jnp.cos(ang)[:, idx], jnp.sin(ang)[:, idx]


def _head_constants():
    d = jnp.arange(LANES)
    same_head = (d[:, None] // ATT_HEAD_DIM) == (d[None, :] // ATT_HEAD_DIM)
    bd = same_head.astype(BF16)
    quarter = ATT_HEAD_DIM // 4
    second = (d % (2 * quarter)) >= quarter
    src = jnp.where(second, d - quarter, d + quarter)
    sign = jnp.where(second, 1.0, -1.0)
    perm = jnp.zeros((LANES, LANES), F32).at[src, d].set(sign).astype(BF16)
    return bd, perm


def _expand_matrix(lane0):
    r = jnp.arange(LANES)[:, None]
    c = jnp.arange(SSD_INNER)[None, :]
    return (r == lane0 + c // SSD_HEAD_DIM).astype(BF16)


def _pad_lanes(v, width=LANES):
    flat = v.reshape(1, -1)
    return jnp.pad(flat, ((0, 0), (0, width - flat.shape[1])))


def _mixer_layer(x, t, cos, sin, bd, perm, e_f, e_b, norm_mix, w_in, conv_w, conv_b, dt_bias, a_log, d_skip,
                 ssd_norm, w_br_ssd, q_norm, k_norm, w_br_att, w_pool, pool_scale, w_br_pool, gate_bias, w_out):
    o_z, o_xbc = 0, SSD_INNER
    o_dt = o_xbc + SSD_CONV_DIM
    o_q = o_dt + 2 * SSD_HEADS
    o_kv = o_q + ATT_WIDTH
    o_pool = o_kv + 2 * ATT_KV_WIDTH
    o_gate = o_pool + POOL_WIDTH
    w_main = jnp.concatenate(
        [w_in[:, o_z:o_dt], w_in[:, o_gate:], w_in[:, o_q:o_kv], w_in[:, o_pool:o_gate], w_in[:, o_kv:o_pool]],
        axis=1).astype(BF16)
    w_dt = jnp.pad(w_in[:, o_dt:o_q], ((0, 0), (0, DT_PAD - 2 * SSD_HEADS))).astype(BF16)

    proj, dt_raw = _in_proj(x, norm_mix.reshape(1, -1), w_main, w_dt, t["inproj_tm"], t["inproj_tn"])

    xbc = _conv(proj, conv_w, conv_b.reshape(1, -1), t["conv_tm"])
    y_f, y_b = _ssd(xbc, dt_raw, _pad_lanes(dt_bias), _pad_lanes(a_log), e_f, e_b)
    b_ssd = _ssd_out(y_f, y_b, xbc, proj, jnp.repeat(d_skip, SSD_HEAD_DIM).reshape(1, -1),
                     ssd_norm.reshape(1, -1), w_br_ssd.astype(BF16), t["tail_tm"])

    qn = jnp.tile(q_norm, LANES // ATT_HEAD_DIM).reshape(1, -1)
    kn = jnp.tile(k_norm, LANES // ATT_HEAD_DIM).reshape(1, -1)
    qt = _q_prep(proj, cos, sin, qn, bd, perm, t["attn_tq"])
    k = _k_prep(proj, cos, sin, kn, bd, perm, t["prep_tm"])
    vt = _v_t(proj, t["attn_tk"])
    att = _attention(qt, k, vt, t["attn_tq"])

    b_pool = _pool(proj, w_pool.astype(BF16), pool_scale.reshape(1, -1), w_br_pool.astype(BF16), t["pool_tm"])
    return _merge(b_ssd, att, b_pool, proj, gate_bias, x, w_br_att.astype(BF16), w_out.astype(BF16), t["tail_tm"])


def _moe_layer(x, t, upper, ltri, norm_ffn, w_rg, b_rg, w_re, b_re, w_gate_e, w_up_e, w_down_e):
    s = x.shape[0]
    rows_per_tile = t["moe_rows"]
    wt = jnp.zeros((LANES, D_MODEL), F32)
    wt = wt.at[ROUTER_GROUP_ROW:ROUTER_GROUP_ROW + MOE_GROUPS].set(w_rg.T)
    wt = wt.at[ROUTER_EXPERT_ROW:ROUTER_EXPERT_ROW + N_EXPERTS].set(w_re.T)
    bias = jnp.zeros((LANES, 1), F32)
    bias = bias.at[ROUTER_GROUP_ROW:ROUTER_GROUP_ROW + MOE_GROUPS, 0].set(b_rg)
    bias = bias.at[ROUTER_EXPERT_ROW:ROUTER_EXPERT_ROW + N_EXPERTS, 0].set(b_re)

    h, eidx, rank, wts, off, tiles = _router(x, norm_ffn.reshape(1, -1), wt, bias, upper, ltri,
                                             t["router_tm"], rows_per_tile)
    pos = _positions(eidx, rank, off).reshape(-1)
    sorted_rows = 2 * s + N_EXPERTS * rows_per_tile
    xs = _dispatch(pos, h, jnp.zeros((sorted_rows, D_MODEL), F32), t["dispatch_tile"])
    ys = _expert_ffn(tiles[0], tiles[1, 0:1], xs, w_gate_e, w_up_e, w_down_e, rows_per_tile)
    return _combine(pos, ys, x, wts.T, t["combine_tile"])


def kernel(x, norm_mix, w_in, conv_w, conv_b, dt_bias, a_log, d_skip, ssd_norm, w_br_ssd, q_norm, k_norm, w_br_att, w_pool, pool_scale, w_br_pool, gate_bias, w_out, norm_ffn, w_router_group, b_router_group, w_router_expert, b_router_expert, w_gate_e, w_up_e, w_down_e):
    b, s, _ = x.shape
    assert b == 1 and s % 1024 == 0 and s % GRID_W == 0
    t = _tiles(s)
    assert 2 * s // t["moe_rows"] + N_EXPERTS <= TILE_TABLE_LANES
    cos, sin = _rope_tables(s)
    bd, perm = _head_constants()
    e_f, e_b = _expand_matrix(0), _expand_matrix(SSD_HEADS)
    rt = t["router_tm"]
    upper = (jnp.arange(rt)[:, None] < jnp.arange(rt)[None, :]).astype(BF16)
    ltri = (jnp.arange(N_EXPERTS)[None, :] < jnp.arange(N_EXPERTS)[:, None]).astype(BF16)

    xx = x[0]
    for l in range(DEPTH):
        xx = _mixer_layer(xx, t, cos, sin, bd, perm, e_f, e_b, norm_mix[l], w_in[l], conv_w[l], conv_b[l], dt_bias[l],
                          a_log[l], d_skip[l], ssd_norm[l], w_br_ssd[l], q_norm[l], k_norm[l], w_br_att[l],
                          w_pool[l], pool_scale[l], w_br_pool[l], gate_bias[l], w_out[l])
        xx = _moe_layer(xx, t, upper, ltri, norm_ffn[l], w_router_group[l], b_router_group[l],
                        w_router_expert[l], b_router_expert[l], w_gate_e[l], w_up_e[l], w_down_e[l])
    return xx[None]
```

```python
import functools
import math

import jax
import jax.numpy as jnp
from jax import lax
from jax.experimental import pallas as pl
from jax.experimental.pallas import tpu as pltpu

F32 = jnp.float32
BF16 = jnp.bfloat16
I32 = jnp.int32
HIGHEST = lax.Precision.HIGHEST

EPS = 1e-6
D_MODEL = 1024
DEPTH = 2

SSD_INNER = 2048
SSD_HEAD_DIM = 64
SSD_HEADS = 32
SSD_GROUPS = 4
SSD_STATE = 128
SSD_CONV = 5
SSD_CHUNK = 128
SSD_BC = SSD_GROUPS * SSD_STATE
SSD_CONV_DIM = SSD_INNER + 2 * SSD_BC
HEADS_PER_GROUP = SSD_HEADS // SSD_GROUPS
GROUP_WIDTH = HEADS_PER_GROUP * SSD_HEAD_DIM

ATT_HEADS = 16
ATT_KV_HEADS = 4
ATT_HEAD_DIM = 64
ATT_WIDTH = 1024
ATT_KV_WIDTH = 256
ATT_REP = ATT_HEADS // ATT_KV_HEADS
ROPE_THETA = 10000.0
GRID_W = 64

POOL_WIDTH = 1024
POOL_WINDOWS = (2, 4, 8, 16)
POOL_GROUP_DIM = 256

N_BRANCH = 3
MOE_GROUPS = 4
MOE_PER_GROUP = 8
N_EXPERTS = 32
EXPERT_FF = 512

LANES = 128
SUBLANES = 8
VMEM_LIMIT_BYTES = 56 * 1024 * 1024

COL_Z = 0
COL_XBC = COL_Z + SSD_INNER
COL_GATES = COL_XBC + SSD_CONV_DIM
COL_Q = COL_GATES + N_BRANCH * D_MODEL
COL_POOL = COL_Q + ATT_WIDTH
COL_K = COL_POOL + POOL_WIDTH
COL_V = COL_K + ATT_KV_WIDTH
PROJ_DIM = COL_V + ATT_KV_WIDTH
DT_PAD = LANES

HALO = 2 * SUBLANES


def _params(*sem):
    return pltpu.CompilerParams(dimension_semantics=sem, vmem_limit_bytes=VMEM_LIMIT_BYTES)


def _silu(v):
    return v * jax.nn.sigmoid(v)


def _split_bf16(v):
    hi = v.astype(BF16)
    lo = (v - hi.astype(F32)).astype(BF16)
    return hi, lo


def _inproj_kernel(x_ref, g_ref, w_ref, wdt_ref, o_ref, dt_ref, h_ref):
    @pl.when(pl.program_id(1) == 0)
    def _():
        x = x_ref[...]
        ms = jnp.mean(x * x, axis=-1, keepdims=True)
        h = (x * lax.rsqrt(ms + EPS) * g_ref[...]).astype(BF16)
        h_ref[...] = h
        dt_ref[...] = jnp.dot(h, wdt_ref[...], preferred_element_type=F32)

    o_ref[...] = jnp.dot(h_ref[...], w_ref[...], preferred_element_type=F32).astype(o_ref.dtype)


def _in_proj(x, g, w, wdt, tm, tn):
    s = x.shape[0]
    return pl.pallas_call(
        _inproj_kernel,
        grid=(s // tm, PROJ_DIM // tn),
        in_specs=[
            pl.BlockSpec((tm, D_MODEL), lambda i, j: (i, 0)),
            pl.BlockSpec((1, D_MODEL), lambda i, j: (0, 0)),
            pl.BlockSpec((D_MODEL, tn), lambda i, j: (0, j)),
            pl.BlockSpec((D_MODEL, DT_PAD), lambda i, j: (0, 0)),
        ],
        out_specs=[
            pl.BlockSpec((tm, tn), lambda i, j: (i, j)),
            pl.BlockSpec((tm, DT_PAD), lambda i, j: (i, 0)),
        ],
        out_shape=[jax.ShapeDtypeStruct((s, PROJ_DIM), BF16), jax.ShapeDtypeStruct((s, DT_PAD), F32)],
        scratch_shapes=[pltpu.VMEM((tm, D_MODEL), BF16)],
        compiler_params=_params("parallel", "arbitrary"),
        name="in_proj",
    )(x, g, w, wdt)


def _fill_ext(ext_ref, main_ref, prev_ref, next_ref, tm):
    i = pl.program_id(0)
    last = pl.num_programs(0) - 1
    ext_ref[0:HALO, :] = jnp.where(i > 0, prev_ref[...].astype(F32), 0.0)
    ext_ref[HALO:HALO + tm, :] = main_ref[...].astype(F32)
    ext_ref[HALO + tm:, :] = jnp.where(i < last, next_ref[...].astype(F32), 0.0)


def _conv_kernel(xm_ref, xp_ref, xn_ref, w_ref, b_ref, o_ref, ext_ref):
    tm = xm_ref.shape[0]
    _fill_ext(ext_ref, xm_ref, xp_ref, xn_ref, tm)
    pad = SSD_CONV // 2
    acc = jnp.broadcast_to(b_ref[...], o_ref.shape)
    for k in range(SSD_CONV):
        acc = acc + ext_ref[pl.ds(HALO - pad + k, tm), :] * w_ref[k:k + 1, :]
    o_ref[...] = _silu(acc).astype(o_ref.dtype)


def _halo_specs(tm, tc, col0, s):
    rb = tm // HALO
    nrb = s // HALO
    return [
        pl.BlockSpec((tm, tc), lambda i, j=0: (i, col0 + j)),
        pl.BlockSpec((HALO, tc), lambda i, j=0: (jnp.maximum(i * rb - 1, 0), col0 + j)),
        pl.BlockSpec((HALO, tc), lambda i, j=0: (jnp.minimum((i + 1) * rb, nrb - 1), col0 + j)),
    ]


def _conv(proj, w, b, tm):
    s = proj.shape[0]
    tc = 1024
    return pl.pallas_call(
        _conv_kernel,
        grid=(s // tm, SSD_CONV_DIM // tc),
        in_specs=_halo_specs(tm, tc, COL_XBC // tc, s) + [
            pl.BlockSpec((SSD_CONV, tc), lambda i, j: (0, j)),
            pl.BlockSpec((1, tc), lambda i, j: (0, j)),
        ],
        out_specs=pl.BlockSpec((tm, tc), lambda i, j: (i, j)),
        out_shape=jax.ShapeDtypeStruct((s, SSD_CONV_DIM), BF16),
        scratch_shapes=[pltpu.VMEM((tm + 2 * HALO, tc), F32)],
        compiler_params=_params("parallel", "parallel"),
        name="conv",
    )(proj, proj, proj, w, b)


def _ssd_direction(x_ref, b_ref, c_ref, dt_ref, e_ref, y_ref, st_ref, bias, a, fwd):
    q = SSD_CHUNK
    row = lax.broadcasted_iota(I32, (q, q), 0)
    col = lax.broadcasted_iota(I32, (q, q), 1)
    mask = (col <= row) if fwd else (col >= row)
    lane0 = 0 if fwd else SSD_HEADS

    dt = jax.nn.softplus(dt_ref[...] + bias)
    da = dt * a
    cs = jnp.dot(mask.astype(F32), da, precision=HIGHEST, preferred_element_type=F32)
    cs_t = cs.T
    edge = q - 1 if fwd else 0
    tot = cs[edge:edge + 1, :]
    dt_hi, dt_lo = _split_bf16(dt)
    dec = jnp.exp(tot)
    dec_hi = dec.astype(BF16).astype(F32)
    r16 = lax.broadcasted_iota(I32, (2 * SUBLANES, LANES), 0)
    dec_rows = jnp.where(r16 == 0, dec_hi, jnp.where(r16 == 1, dec - dec_hi, 0.0)).astype(BF16)
    lhs = jnp.concatenate(
        [dt_hi, dt_lo, jnp.exp(cs).astype(BF16), jnp.exp(tot - cs).astype(BF16), dec_rows], axis=0)
    ex = jnp.dot(lhs, e_ref[...], preferred_element_type=F32)
    dt_x = ex[0:q] + ex[q:2 * q]
    ecs_x = ex[2 * q:3 * q]
    dte_x = ex[3 * q:4 * q]
    dec_x = ex[4 * q:4 * q + 1] + ex[4 * q + 1:4 * q + 2]

    xdt = x_ref[...].astype(F32) * dt_x
    xdte_b = (xdt * dte_x).astype(BF16)
    lane = lax.broadcasted_iota(I32, (q, LANES), 1)
    first_half = lane < SSD_HEAD_DIM
    bm = b_ref[...].astype(BF16)
    cm = c_ref[...].astype(BF16)
    for g in range(SSD_GROUPS):
        bg = bm[:, g * SSD_STATE:(g + 1) * SSD_STATE]
        cg = cm[:, g * SSD_STATE:(g + 1) * SSD_STATE]
        cb = lax.dot_general(cg, bg, (((1,), (1,)), ((), ())), preferred_element_type=F32)
        gs = slice(g * GROUP_WIDTH, (g + 1) * GROUP_WIDTH)
        st = st_ref[g]
        y_off = jnp.dot(cg, st.astype(BF16), preferred_element_type=F32) * ecs_x[:, gs]
        for pr in range(HEADS_PER_GROUP // 2):
            c0 = g * GROUP_WIDTH + pr * LANES
            xp = xdt[:, c0:c0 + LANES]
            halves = (jnp.where(first_half, xp, 0.0).astype(BF16), jnp.where(first_half, 0.0, xp).astype(BF16))
            yd = None
            for k in range(2):
                hl = lane0 + g * HEADS_PER_GROUP + 2 * pr + k
                seg = cs[:, hl:hl + 1] - cs_t[hl:hl + 1, :]
                decay = jnp.exp(jnp.where(mask, seg, -jnp.inf))
                part = jnp.dot((cb * decay).astype(BF16), halves[k], preferred_element_type=F32)
                yd = part if yd is None else yd + part
            y_ref[:, c0:c0 + LANES] = (yd + y_off[:, pr * LANES:(pr + 1) * LANES]).astype(y_ref.dtype)
        upd = lax.dot_general(bg, xdte_b[:, gs], (((0,), (0,)), ((), ())), preferred_element_type=F32)
        st_ref[g] = st * dec_x[:, gs] + upd


def _ssd_kernel(xf_ref, bf_ref, cf_ref, dtf_ref, xb_ref, bb_ref, cb_ref, dtb_ref,
                bias_ref, alog_ref, ef_ref, eb_ref, yf_ref, yb_ref, stf_ref, stb_ref):
    @pl.when(pl.program_id(0) == 0)
    def _():
        stf_ref[...] = jnp.zeros_like(stf_ref)
        stb_ref[...] = jnp.zeros_like(stb_ref)

    bias = bias_ref[...]
    a = -jnp.exp(alog_ref[...])
    _ssd_direction(xf_ref, bf_ref, cf_ref, dtf_ref, ef_ref, yf_ref, stf_ref, bias, a, True)
    _ssd_direction(xb_ref, bb_ref, cb_ref, dtb_ref, eb_ref, yb_ref, stb_ref, bias, a, False)


def _ssd(xbc, dt_raw, dt_bias, a_log, e_f, e_b):
    s = xbc.shape[0]
    q = SSD_CHUNK
    nc = s // q
    nb = SSD_INNER // SSD_BC

    def chunk_specs(im):
        return [
            pl.BlockSpec((q, SSD_INNER), lambda i: (im(i), 0)),
            pl.BlockSpec((q, SSD_BC), lambda i: (im(i), nb)),
            pl.BlockSpec((q, SSD_BC), lambda i: (im(i), nb + 1)),
            pl.BlockSpec((q, DT_PAD), lambda i: (im(i), 0)),
        ]

    fwd = lambda i: i
    bwd = lambda i: nc - 1 - i
    const = lambda shape: pl.BlockSpec(shape, lambda i: (0,) * len(shape))
    st_shape = (SSD_GROUPS, SSD_STATE, GROUP_WIDTH)
    return pl.pallas_call(
        _ssd_kernel,
        grid=(nc,),
        in_specs=chunk_specs(fwd) + chunk_specs(bwd) + [
            const((1, DT_PAD)), const((1, DT_PAD)), const((LANES, SSD_INNER)), const((LANES, SSD_INNER))],
        out_specs=[pl.BlockSpec((q, SSD_INNER), lambda i: (fwd(i), 0)),
                   pl.BlockSpec((q, SSD_INNER), lambda i: (bwd(i), 0))],
        out_shape=[jax.ShapeDtypeStruct((s, SSD_INNER), BF16)] * 2,
        scratch_shapes=[pltpu.VMEM(st_shape, F32), pltpu.VMEM(st_shape, F32)],
        compiler_params=_params("arbitrary"),
        name="ssd",
    )(xbc, xbc, xbc, dt_raw, xbc, xbc, xbc, dt_raw, dt_bias, a_log, e_f, e_b)


def _norm_rope(x, cos, sin, g, bd, perm):
    hi, lo = _split_bf16(x * x)
    ss = jnp.dot(hi, bd, preferred_element_type=F32) + jnp.dot(lo, bd, preferred_element_type=F32)
    y = x * lax.rsqrt(ss * (1.0 / ATT_HEAD_DIM) + EPS) * g
    yh, yl = _split_bf16(y)
    rot = jnp.dot(yh, perm, preferred_element_type=F32) + jnp.dot(yl, perm, preferred_element_type=F32)
    return y * cos + rot * sin


def _qprep_kernel(x_ref, cos_ref, sin_ref, g_ref, bd_ref, perm_ref, o_ref, *, scale):
    tq = x_ref.shape[0]
    cos, sin, g, bd, perm = cos_ref[...], sin_ref[...], g_ref[...], bd_ref[...], perm_ref[...]
    for j in range(ATT_WIDTH // LANES):
        out = _norm_rope(x_ref[:, j * LANES:(j + 1) * LANES].astype(F32), cos, sin, g, bd, perm) * scale
        out_t = out.T.astype(BF16)
        grp, r0 = (2 * j) // ATT_REP, (2 * j) % ATT_REP
        o_ref[grp, :, r0 * tq:(r0 + 1) * tq] = out_t[:ATT_HEAD_DIM]
        o_ref[grp, :, (r0 + 1) * tq:(r0 + 2) * tq] = out_t[ATT_HEAD_DIM:]


def _kprep_kernel(x_ref, cos_ref, sin_ref, g_ref, bd_ref, perm_ref, o_ref):
    out = _norm_rope(x_ref[...].astype(F32), cos_ref[...], sin_ref[...], g_ref[...], bd_ref[...],
                     perm_ref[...]).astype(BF16)
    o_ref[0] = out[:, :ATT_HEAD_DIM]
    o_ref[1] = out[:, ATT_HEAD_DIM:]


VT_ROWS = ATT_HEAD_DIM + 2 * SUBLANES


def _vt_kernel(x_ref, o_ref):
    xt = x_ref[...].astype(F32).T.astype(BF16)
    pad = VT_ROWS - ATT_HEAD_DIM
    ones_row = (lax.broadcasted_iota(I32, (pad, xt.shape[1]), 0) == 0).astype(BF16)
    for g in range(ATT_KV_HEADS):
        o_ref[g * VT_ROWS:g * VT_ROWS + ATT_HEAD_DIM, :] = xt[g * ATT_HEAD_DIM:(g + 1) * ATT_HEAD_DIM]
        o_ref[g * VT_ROWS + ATT_HEAD_DIM:(g + 1) * VT_ROWS, :] = ones_row


def _rope_specs(tm):
    return [
        pl.BlockSpec((tm, LANES), lambda i, j: (i, 0)),
        pl.BlockSpec((tm, LANES), lambda i, j: (i, 0)),
        pl.BlockSpec((1, LANES), lambda i, j: (0, 0)),
        pl.BlockSpec((LANES, LANES), lambda i, j: (0, 0)),
        pl.BlockSpec((LANES, LANES), lambda i, j: (0, 0)),
    ]


def _q_prep(proj, cos, sin, g, bd, perm, tq):
    s = proj.shape[0]
    scale = ATT_HEAD_DIM ** -0.5 * math.log2(math.e)
    const = lambda i: (0, 0)
    return pl.pallas_call(
        functools.partial(_qprep_kernel, scale=scale),
        grid=(s // tq,),
        in_specs=[
            pl.BlockSpec((tq, ATT_WIDTH), lambda i: (i, COL_Q // ATT_WIDTH)),
            pl.BlockSpec((tq, LANES), lambda i: (i, 0)),
            pl.BlockSpec((tq, LANES), lambda i: (i, 0)),
            pl.BlockSpec((1, LANES), const),
            pl.BlockSpec((LANES, LANES), const),
            pl.BlockSpec((LANES, LANES), const),
        ],
        out_specs=pl.BlockSpec((ATT_KV_HEADS, ATT_HEAD_DIM, ATT_REP * tq), lambda i: (0, 0, i)),
        out_shape=jax.ShapeDtypeStruct((ATT_KV_HEADS, ATT_HEAD_DIM, ATT_REP * s), BF16),
        compiler_params=_params("parallel"),
        name="q_prep",
    )(proj, cos, sin, g, bd, perm)


def _k_prep(proj, cos, sin, g, bd, perm, tm):
    s = proj.shape[0]
    return pl.pallas_call(
        _kprep_kernel,
        grid=(s // tm, ATT_KV_WIDTH // LANES),
        in_specs=[pl.BlockSpec((tm, LANES), lambda i, j: (i, COL_K // LANES + j))] + _rope_specs(tm),
        out_specs=pl.BlockSpec((2, tm, ATT_HEAD_DIM), lambda i, j: (j, i, 0)),
        out_shape=jax.ShapeDtypeStruct((ATT_KV_HEADS, s, ATT_HEAD_DIM), BF16),
        compiler_params=_params("parallel", "parallel"),
        name="k_prep",
    )(proj, cos, sin, g, bd, perm)


def _v_t(proj, tk):
    s = proj.shape[0]
    rows = ATT_KV_HEADS * VT_ROWS
    return pl.pallas_call(
        _vt_kernel,
        grid=(s // tk,),
        in_specs=[pl.BlockSpec((tk, ATT_KV_WIDTH), lambda i: (i, COL_V // ATT_KV_WIDTH))],
        out_specs=pl.BlockSpec((None, rows, tk), lambda i: (i, 0, 0)),
        out_shape=jax.ShapeDtypeStruct((s // tk, rows, tk), BF16),
        compiler_params=_params("parallel"),
        name="v_t",
    )(proj)


SAFE_SCORE_SPAN = 100.0


FAST_SWEEP_UNROLL = 4


def _attn_kernel(qt_ref, k_ref, vt_ref, o_ref, kmax_ref, shift_ref, acc_ref, pa_ref, pb_ref, sa_ref, sb_ref, *, tk):
    s = k_ref.shape[0]
    nq = qt_ref.shape[1]
    n_chunks = s // tk

    @pl.when(pl.program_id(1) == 0)
    def _():
        def key_norm(c, best):
            kc = k_ref[pl.ds(pl.multiple_of(c * tk, tk), tk), :].astype(F32)
            norm = jnp.sum(kc * kc, axis=1, keepdims=True)
            return jnp.maximum(best, jnp.max(norm, axis=0, keepdims=True))

        best = lax.fori_loop(0, n_chunks, key_norm, jnp.zeros((1, 1), F32))
        kmax_ref[...] = jnp.broadcast_to(best, kmax_ref.shape)

    q = qt_ref[...].astype(F32)
    bound = jnp.sqrt(jnp.sum(q * q, axis=0, keepdims=True) * kmax_ref[0:1, 0:1])
    safe = 2.0 * jnp.max(bound) <= SAFE_SCORE_SPAN
    acc_ref[...] = jnp.zeros_like(acc_ref)

    def scores(c):
        kc = k_ref[pl.ds(pl.multiple_of(c * tk, tk), tk), :]
        return jnp.dot(kc, qt_ref[...], preferred_element_type=F32)

    def sweep(produce, consume, buf_a, buf_b, unroll):
        produce(0, buf_a)

        def body(i, carry):
            c = 2 * i
            produce(c + 1, buf_b)
            consume(c, buf_a)
            produce(jnp.minimum(c + 2, n_chunks - 1), buf_a)
            consume(c + 1, buf_b)
            return carry

        lax.fori_loop(0, n_chunks // 2, body, 0, unroll=unroll)

    @pl.when(safe)
    def _():
        def produce(c, dst_ref):
            dst_ref[...] = jnp.exp2(scores(c) - bound).astype(BF16)

        def consume(c, src_ref):
            acc_ref[...] += jnp.dot(vt_ref[c], src_ref[...], preferred_element_type=F32)

        sweep(produce, consume, pa_ref, pb_ref, unroll=FAST_SWEEP_UNROLL)

    @pl.when(jnp.logical_not(safe))
    def _():
        shift_ref[...] = jnp.full_like(shift_ref, -jnp.inf)

        def produce(c, dst_ref):
            dst_ref[...] = scores(c)

        def consume(c, src_ref):
            st = src_ref[...]
            m_old = shift_ref[...]
            m_new = jnp.maximum(m_old, jnp.max(st, axis=0, keepdims=True))
            p = jnp.exp2(st - m_new).astype(BF16)
            acc_ref[...] = jnp.exp2(m_old - m_new) * acc_ref[...] + jnp.dot(vt_ref[c], p, preferred_element_type=F32)
            shift_ref[...] = m_new

        sweep(produce, consume, sa_ref, sb_ref, unroll=1)

    acc = acc_ref[...]
    out = acc[:ATT_HEAD_DIM] / acc[ATT_HEAD_DIM:ATT_HEAD_DIM + 1]
    tq = nq // ATT_REP
    heads = [out[:, r * tq:(r + 1) * tq] for r in range(ATT_REP)]
    o_ref[...] = jnp.concatenate(heads, axis=0).T.astype(o_ref.dtype)


def _attention(qt, k, vt, tq):
    s = k.shape[1]
    tk = vt.shape[2]
    nq = ATT_REP * tq
    return pl.pallas_call(
        functools.partial(_attn_kernel, tk=tk),
        grid=(ATT_KV_HEADS, s // tq),
        in_specs=[
            pl.BlockSpec((None, ATT_HEAD_DIM, nq), lambda g, i: (g, 0, i)),
            pl.BlockSpec((None, s, ATT_HEAD_DIM), lambda g, i: (g, 0, 0)),
            pl.BlockSpec((s // tk, VT_ROWS, tk), lambda g, i: (0, g, 0)),
        ],
        out_specs=pl.BlockSpec((tq, ATT_REP * ATT_HEAD_DIM), lambda g, i: (i, g)),
        out_shape=jax.ShapeDtypeStruct((s, ATT_WIDTH), BF16),
        scratch_shapes=[pltpu.VMEM((SUBLANES, LANES), F32), pltpu.VMEM((1, nq), F32), pltpu.VMEM((VT_ROWS, nq), F32),
                        pltpu.VMEM((tk, nq), BF16), pltpu.VMEM((tk, nq), BF16),
                        pltpu.VMEM((tk, nq), F32), pltpu.VMEM((tk, nq), F32)],
        compiler_params=_params("parallel", "arbitrary"),
        name="attention",
    )(qt, k, vt)


def _ssd_out_kernel(yf_ref, yb_ref, xs_ref, z_ref, dskip_ref, g_ref, w_ref, o_ref):
    y = yf_ref[...].astype(F32) + yb_ref[...].astype(F32) + xs_ref[...].astype(F32) * dskip_ref[...]
    y = y * _silu(z_ref[...].astype(F32))
    ms = jnp.mean(y * y, axis=-1, keepdims=True)
    yn = (y * lax.rsqrt(ms + EPS) * g_ref[...]).astype(BF16)
    o_ref[...] = jnp.dot(yn, w_ref[...], preferred_element_type=F32).astype(o_ref.dtype)


def _ssd_out(yf, yb, xbc, proj, dskip, g, w, tm):
    s = yf.shape[0]
    row = lambda i: (i, 0)
    const = lambda i: (0, 0)
    return pl.pallas_call(
        _ssd_out_kernel,
        grid=(s // tm,),
        in_specs=[
            pl.BlockSpec((tm, SSD_INNER), row), pl.BlockSpec((tm, SSD_INNER), row),
            pl.BlockSpec((tm, SSD_INNER), row), pl.BlockSpec((tm, SSD_INNER), row),
            pl.BlockSpec((1, SSD_INNER), const), pl.BlockSpec((1, SSD_INNER), const),
            pl.BlockSpec((SSD_INNER, D_MODEL), const),
        ],
        out_specs=pl.BlockSpec((tm, D_MODEL), row),
        out_shape=jax.ShapeDtypeStruct((s, D_MODEL), BF16),
        compiler_params=_params("parallel"),
        name="ssd_out",
    )(yf, yb, xbc, proj, dskip, g, w)


def _pool_kernel(um_ref, up_ref, un_ref, wp_ref, scale_ref, w_ref, o_ref, ext_ref, *, seq):
    tm = um_ref.shape[0]
    _fill_ext(ext_ref, um_ref, up_ref, un_ref, tm)
    t = pl.program_id(0) * tm + lax.broadcasted_iota(I32, (tm, 1), 0)
    mixed = []
    for gi, win in enumerate(POOL_WINDOWS):
        half = win // 2
        cols = slice(gi * POOL_GROUP_DIM, (gi + 1) * POOL_GROUP_DIM)
        acc = ext_ref[pl.ds(HALO - half, tm), cols]
        for k in range(1, win):
            acc = acc + ext_ref[pl.ds(HALO - half + k, tm), cols]
        cnt = (jnp.minimum(t + half, seq) - jnp.maximum(t - half, 0)).astype(F32)
        mean = acc / cnt
        mix = (mean - ext_ref[pl.ds(HALO, tm), cols]).astype(BF16)
        mixed.append(jnp.dot(mix, wp_ref[gi], preferred_element_type=F32))
    pooled = (jnp.concatenate(mixed, axis=1) * scale_ref[...]).astype(BF16)
    o_ref[...] = jnp.dot(pooled, w_ref[...], preferred_element_type=F32).astype(o_ref.dtype)


def _pool(proj, w_pool, scale, w_br, tm):
    s = proj.shape[0]
    return pl.pallas_call(
        functools.partial(_pool_kernel, seq=s),
        grid=(s // tm,),
        in_specs=_halo_specs(tm, POOL_WIDTH, COL_POOL // POOL_WIDTH, s) + [
            pl.BlockSpec((len(POOL_WINDOWS), POOL_GROUP_DIM, POOL_GROUP_DIM), lambda i: (0, 0, 0)),
            pl.BlockSpec((1, POOL_WIDTH), lambda i: (0, 0)),
            pl.BlockSpec((POOL_WIDTH, D_MODEL), lambda i: (0, 0)),
        ],
        out_specs=pl.BlockSpec((tm, D_MODEL), lambda i: (i, 0)),
        out_shape=jax.ShapeDtypeStruct((s, D_MODEL), BF16),
        scratch_shapes=[pltpu.VMEM((tm + 2 * HALO, POOL_WIDTH), F32)],
        compiler_params=_params("parallel"),
        name="pool",
    )(proj, proj, proj, w_pool, scale, w_br)


def _merge_kernel(bs_ref, att_ref, bp_ref, g0_ref, g1_ref, g2_ref, gb_ref, x_ref, watt_ref, wout_ref, o_ref):
    b_att = jnp.dot(att_ref[...], watt_ref[...], preferred_element_type=F32)
    gb = gb_ref[...]
    merged = (jax.nn.sigmoid(g0_ref[...] + gb[0:1]) * bs_ref[...]
              + jax.nn.sigmoid(g1_ref[...] + gb[1:2]) * b_att
              + jax.nn.sigmoid(g2_ref[...] + gb[2:3]) * bp_ref[...])
    o_ref[...] = x_ref[...] + jnp.dot(merged.astype(BF16), wout_ref[...], preferred_element_type=F32)


def _merge(b_ssd, att, b_pool, proj, gate_bias, x, w_att, w_out, tm):
    s = x.shape[0]
    row = lambda i: (i, 0)
    const = lambda i: (0, 0)
    gcol = COL_GATES // D_MODEL
    tile = pl.BlockSpec((tm, D_MODEL), row)
    return pl.pallas_call(
        _merge_kernel,
        grid=(s // tm,),
        in_specs=[
            tile, tile, tile,
            pl.BlockSpec((tm, D_MODEL), lambda i: (i, gcol)),
            pl.BlockSpec((tm, D_MODEL), lambda i: (i, gcol + 1)),
            pl.BlockSpec((tm, D_MODEL), lambda i: (i, gcol + 2)),
            pl.BlockSpec((N_BRANCH, D_MODEL), const),
            tile,
            pl.BlockSpec((ATT_WIDTH, D_MODEL), const),
            pl.BlockSpec((D_MODEL, D_MODEL), const),
        ],
        out_specs=tile,
        out_shape=jax.ShapeDtypeStruct((s, D_MODEL), F32),
        compiler_params=_params("parallel"),
        name="merge",
    )(b_ssd, att, b_pool, proj, proj, proj, gate_bias, x, w_att, w_out)


ROUTER_GROUP_ROW = 0
ROUTER_EXPERT_ROW = SUBLANES
TILE_TABLE_LANES = 256


def _first_argmax(v, idx, big):
    m = jnp.max(v, axis=0, keepdims=True)
    return m, jnp.min(jnp.where(v == m, idx, big), axis=0, keepdims=True)


def _router_kernel(x_ref, g_ref, wt_ref, b_ref, upper_ref, ltri_ref,
                   h_ref, eidx_ref, rank_ref, wts_ref, off_ref, tiles_ref, count_ref, *, tile_rows):
    i = pl.program_id(0)
    tm = x_ref.shape[0]

    @pl.when(i == 0)
    def _():
        count_ref[...] = jnp.zeros_like(count_ref)

    x = x_ref[...]
    ms = jnp.mean(x * x, axis=-1, keepdims=True)
    h = x * lax.rsqrt(ms + EPS) * g_ref[...]
    h_ref[...] = h
    lt = lax.dot_general(wt_ref[...], h, (((1,), (1,)), ((), ())), precision=HIGHEST,
                         preferred_element_type=F32) + b_ref[...]

    r8 = lax.broadcasted_iota(I32, (SUBLANES, tm), 0)
    gl = jnp.where(r8 < MOE_GROUPS, lt[ROUTER_GROUP_ROW:ROUTER_GROUP_ROW + SUBLANES], -jnp.inf)
    gmax, gsel = _first_argmax(gl, r8, SUBLANES)
    p_group = 1.0 / jnp.sum(jnp.exp(gl - gmax), axis=0, keepdims=True)

    r32 = lax.broadcasted_iota(I32, (N_EXPERTS, tm), 0)
    el = lt[ROUTER_EXPERT_ROW:ROUTER_EXPERT_ROW + N_EXPERTS]
    el = jnp.where(lax.shift_right_logical(r32, int(math.log2(MOE_PER_GROUP))) == gsel, el, -jnp.inf)
    m1, i1 = _first_argmax(el, r32, N_EXPERTS)
    el2 = jnp.where(r32 == i1, -jnp.inf, el)
    m2, i2 = _first_argmax(el2, r32, N_EXPERTS)
    z = jnp.sum(jnp.exp(el - m1), axis=0, keepdims=True)
    p1 = 1.0 / z
    p2 = jnp.exp(m2 - m1) / z
    psum = p1 + p2
    wts_ref[0:1, :] = p_group * (p1 / psum)
    wts_ref[1:2, :] = p_group * (p2 / psum)
    eidx_ref[0:1, :] = i1
    eidx_ref[1:2, :] = i2

    sel0 = r32 == i1
    sel1 = r32 == i2
    member = jnp.where(sel0 | sel1, 1.0, 0.0)
    before = jnp.dot(member.astype(BF16), upper_ref[...], preferred_element_type=F32) + count_ref[:, 0:1]
    rank_ref[0:1, :] = jnp.sum(jnp.where(sel0, before, 0.0), axis=0, keepdims=True).astype(I32)
    rank_ref[1:2, :] = jnp.sum(jnp.where(sel1, before, 0.0), axis=0, keepdims=True).astype(I32)
    count_ref[...] = count_ref[...] + jnp.sum(member, axis=1, keepdims=True)

    @pl.when(i == pl.num_programs(0) - 1)
    def _():
        cnt = count_ref[...]
        ntile = jnp.floor((cnt + (tile_rows - 1)) * (1.0 / tile_rows))
        start = jnp.dot(ltri_ref[...], ntile.astype(BF16), preferred_element_type=F32)
        off_ref[...] = (start * tile_rows).astype(I32)
        end = (start + ntile)[:, 0:1]
        tid = lax.broadcasted_iota(I32, (N_EXPERTS, TILE_TABLE_LANES), 1).astype(F32)
        owner = jnp.sum(jnp.where(end <= tid, 1.0, 0.0), axis=0, keepdims=True)
        owner = jnp.minimum(owner, N_EXPERTS - 1.0)
        nact = jnp.sum(ntile[:, 0:1], axis=0, keepdims=True)
        r = lax.broadcasted_iota(I32, (SUBLANES, TILE_TABLE_LANES), 0)
        tiles_ref[...] = jnp.where(r == 0, owner, nact).astype(I32)


def _router(x, g, wt, b, upper, ltri, tm, tile_rows):
    s = x.shape[0]
    row = lambda i: (i, 0)
    const = lambda i: (0, 0)
    lane = lambda i: (0, i)
    return pl.pallas_call(
        functools.partial(_router_kernel, tile_rows=tile_rows),
        grid=(s // tm,),
        in_specs=[
            pl.BlockSpec((tm, D_MODEL), row),
            pl.BlockSpec((1, D_MODEL), const),
            pl.BlockSpec((LANES, D_MODEL), const),
            pl.BlockSpec((LANES, 1), const),
            pl.BlockSpec((tm, tm), const),
            pl.BlockSpec((N_EXPERTS, N_EXPERTS), const),
        ],
        out_specs=[
            pl.BlockSpec((tm, D_MODEL), row),
            pl.BlockSpec((2, tm), lane),
            pl.BlockSpec((2, tm), lane),
            pl.BlockSpec((2, tm), lane),
            pl.BlockSpec((N_EXPERTS, LANES), const),
            pl.BlockSpec((SUBLANES, TILE_TABLE_LANES), const),
        ],
        out_shape=[
            jax.ShapeDtypeStruct((s, D_MODEL), F32),
            jax.ShapeDtypeStruct((2, s), I32),
            jax.ShapeDtypeStruct((2, s), I32),
            jax.ShapeDtypeStruct((2, s), F32),
            jax.ShapeDtypeStruct((N_EXPERTS, LANES), I32),
            jax.ShapeDtypeStruct((SUBLANES, TILE_TABLE_LANES), I32),
        ],
        scratch_shapes=[pltpu.VMEM((N_EXPERTS, LANES), F32)],
        compiler_params=_params("arbitrary"),
        name="router",
    )(x, g, wt, b, upper, ltri)


def _positions_kernel(eidx_ref, rank_ref, off_ref, pos_ref):
    e = eidx_ref[...]
    pos = rank_ref[...]
    for x in range(N_EXPERTS):
        pos = pos + jnp.where(e == x, off_ref[x:x + 1, 0:1], 0)
    pos_ref[...] = pos


def _positions(eidx, rank, off):
    full = lambda a: pl.BlockSpec(a.shape, lambda i: (0, 0))
    return pl.pallas_call(
        _positions_kernel,
        grid=(1,),
        in_specs=[full(eidx), full(rank), full(off)],
        out_specs=full(eidx),
        out_shape=jax.ShapeDtypeStruct(eidx.shape, I32),
        compiler_params=_params("arbitrary"),
        name="positions",
    )(eidx, rank, off)


ROW_DMA_UNROLL = 8


def _dispatch_kernel(pos_s, h_ref, xs_in_hbm, xs_hbm, sem, *, seq):
    del xs_in_hbm
    tile = h_ref.shape[0]
    base = pl.program_id(0) * tile

    def row_copy(t, slot):
        pos = pos_s[slot * seq + base + t]
        return pltpu.make_async_copy(h_ref.at[pl.ds(t, 1), :], xs_hbm.at[pl.ds(pos, 1), :], sem)

    def start(t, c):
        row_copy(t, 0).start()
        row_copy(t, 1).start()
        return c

    def wait(t, c):
        row_copy(t, 0).wait()
        row_copy(t, 1).wait()
        return c

    lax.fori_loop(0, tile, start, 0, unroll=ROW_DMA_UNROLL)
    lax.fori_loop(0, tile, wait, 0, unroll=ROW_DMA_UNROLL)


def _dispatch(pos, h, xs_zero, tile):
    s = h.shape[0]
    return pl.pallas_call(
        functools.partial(_dispatch_kernel, seq=s),
        grid_spec=pltpu.PrefetchScalarGridSpec(
            num_scalar_prefetch=1,
            grid=(s // tile,),
            in_specs=[pl.BlockSpec((tile, D_MODEL), lambda i, *_: (i, 0)), pl.BlockSpec(memory_space=pl.ANY)],
            out_specs=pl.BlockSpec(memory_space=pl.ANY),
            scratch_shapes=[pltpu.SemaphoreType.DMA],
        ),
        out_shape=jax.ShapeDtypeStruct(xs_zero.shape, xs_zero.dtype),
        input_output_aliases={2: 0},
        compiler_params=_params("arbitrary"),
        name="dispatch",
    )(pos, h, xs_zero)


def _ffn_kernel(owner_s, nact_s, x_ref, wg_ref, wu_ref, wd_ref, o_ref):
    j = pl.program_id(0)

    @pl.when(j < nact_s[0])
    def _():
        x = x_ref[...].astype(BF16)
        gate = jnp.dot(x, wg_ref[...].astype(BF16), preferred_element_type=F32)
        up = jnp.dot(x, wu_ref[...].astype(BF16), preferred_element_type=F32)
        hid = (_silu(gate) * up).astype(BF16)
        o_ref[...] = jnp.dot(hid, wd_ref[...].astype(BF16), preferred_element_type=F32)

    @pl.when(j >= nact_s[0])
    def _():
        o_ref[...] = jnp.zeros_like(o_ref)


def _expert_ffn(owner, nact, xs, wg, wu, wd, layer, tile_rows):
    rows = xs.shape[0]
    return pl.pallas_call(
        _ffn_kernel,
        grid_spec=pltpu.PrefetchScalarGridSpec(
            num_scalar_prefetch=2,
            grid=(rows // tile_rows,),
            in_specs=[
                pl.BlockSpec((tile_rows, D_MODEL), lambda j, o, n: (j, 0)),
                pl.BlockSpec((None, None, D_MODEL, EXPERT_FF), lambda j, o, n: (layer, o[j], 0, 0)),
                pl.BlockSpec((None, None, D_MODEL, EXPERT_FF), lambda j, o, n: (layer, o[j], 0, 0)),
                pl.BlockSpec((None, None, EXPERT_FF, D_MODEL), lambda j, o, n: (layer, o[j], 0, 0)),
            ],
            out_specs=pl.BlockSpec((tile_rows, D_MODEL), lambda j, o, n: (j, 0)),
        ),
        out_shape=jax.ShapeDtypeStruct((rows, D_MODEL), F32),
        compiler_params=_params("arbitrary"),
        name="expert_ffn",
    )(owner, nact, xs, wg, wu, wd)


def _combine_kernel(pos_s, ys_hbm, x_ref, w_ref, o_ref, buf_ref, sem, *, seq):
    tile = x_ref.shape[0]
    base = pl.program_id(0) * tile

    def row_copy(t, slot):
        pos = pos_s[slot * seq + base + t]
        return pltpu.make_async_copy(ys_hbm.at[pl.ds(pos, 1), :], buf_ref.at[slot, pl.ds(t, 1), :], sem)

    def start(t, c):
        row_copy(t, 0).start()
        row_copy(t, 1).start()
        return c

    def wait(t, c):
        row_copy(t, 0).wait()
        row_copy(t, 1).wait()
        return c

    lax.fori_loop(0, tile, start, 0, unroll=ROW_DMA_UNROLL)
    lax.fori_loop(0, tile, wait, 0, unroll=ROW_DMA_UNROLL)
    w = w_ref[...]
    o_ref[...] = x_ref[...] + w[:, 0:1] * buf_ref[0] + w[:, 1:2] * buf_ref[1]


def _combine(pos, ys, x, w_cols, tile):
    s = x.shape[0]
    return pl.pallas_call(
        functools.partial(_combine_kernel, seq=s),
        grid_spec=pltpu.PrefetchScalarGridSpec(
            num_scalar_prefetch=1,
            grid=(s // tile,),
            in_specs=[
                pl.BlockSpec(memory_space=pl.ANY),
                pl.BlockSpec((tile, D_MODEL), lambda i, *_: (i, 0)),
                pl.BlockSpec((tile, 2), lambda i, *_: (i, 0)),
            ],
            out_specs=pl.BlockSpec((tile, D_MODEL), lambda i, *_: (i, 0)),
            scratch_shapes=[pltpu.VMEM((2, tile, D_MODEL), F32), pltpu.SemaphoreType.DMA],
        ),
        out_shape=jax.ShapeDtypeStruct((s, D_MODEL), F32),
        compiler_params=_params("arbitrary"),
        name="combine",
    )(pos, ys, x, w_cols)


def _tiles(s):
    return dict(
        inproj_tm=min(1024, s), inproj_tn=768,
        conv_tm=min(512, s), prep_tm=min(1024, s),
        attn_tq=min(512, s), attn_tk=256,
        tail_tm=min(256, s), pool_tm=min(512, s),
        router_tm=min(512, s), moe_rows=256, dispatch_tile=min(512, s), combine_tile=min(256, s),
    )


def _rope_tables(s):
    rows = s // GRID_W
    row = jnp.repeat(jnp.arange(rows), GRID_W).astype(F32)
    col = jnp.tile(jnp.arange(GRID_W), rows).astype(F32)
    half = ATT_HEAD_DIM // 2
    inv_freq = ROPE_THETA ** (-jnp.arange(0, half, 2, dtype=F32) / half)
    ang = jnp.concatenate([row[:, None] * inv_freq, col[:, None] * inv_freq], axis=-1)
    d = jnp.arange(LANES)
    idx = ((d % ATT_HEAD_DIM) // half) * (half // 2) + d % (half // 2)
    return jnp.cos(ang)[:, idx], jnp.sin(ang)[:, idx]


def _head_constants():
    d = jnp.arange(LANES)
    same_head = (d[:, None] // ATT_HEAD_DIM) == (d[None, :] // ATT_HEAD_DIM)
    bd = same_head.astype(BF16)
    quarter = ATT_HEAD_DIM // 4
    second = (d % (2 * quarter)) >= quarter
    src = jnp.where(second, d - quarter, d + quarter)
    sign = jnp.where(second, 1.0, -1.0)
    perm = jnp.zeros((LANES, LANES), F32).at[src, d].set(sign).astype(BF16)
    return bd, perm


def _expand_matrix(lane0):
    r = jnp.arange(LANES)[:, None]
    c = jnp.arange(SSD_INNER)[None, :]
    return (r == lane0 + c // SSD_HEAD_DIM).astype(BF16)


def _pad_lanes(v, width=LANES):
    flat = v.reshape(1, -1)
    return jnp.pad(flat, ((0, 0), (0, width - flat.shape[1])))


def _mixer_layer(x, t, cos, sin, bd, perm, e_f, e_b, norm_mix, w_in, conv_w, conv_b, dt_bias, a_log, d_skip,
                 ssd_norm, w_br_ssd, q_norm, k_norm, w_br_att, w_pool, pool_scale, w_br_pool, gate_bias, w_out):
    o_z, o_xbc = 0, SSD_INNER
    o_dt = o_xbc + SSD_CONV_DIM
    o_q = o_dt + 2 * SSD_HEADS
    o_kv = o_q + ATT_WIDTH
    o_pool = o_kv + 2 * ATT_KV_WIDTH
    o_gate = o_pool + POOL_WIDTH
    w_main = jnp.concatenate(
        [w_in[:, o_z:o_dt], w_in[:, o_gate:], w_in[:, o_q:o_kv], w_in[:, o_pool:o_gate], w_in[:, o_kv:o_pool]],
        axis=1).astype(BF16)
    w_dt = jnp.pad(w_in[:, o_dt:o_q], ((0, 0), (0, DT_PAD - 2 * SSD_HEADS))).astype(BF16)

    proj, dt_raw = _in_proj(x, norm_mix.reshape(1, -1), w_main, w_dt, t["inproj_tm"], t["inproj_tn"])

    xbc = _conv(proj, conv_w, conv_b.reshape(1, -1), t["conv_tm"])
    y_f, y_b = _ssd(xbc, dt_raw, _pad_lanes(dt_bias), _pad_lanes(a_log), e_f, e_b)
    b_ssd = _ssd_out(y_f, y_b, xbc, proj, jnp.repeat(d_skip, SSD_HEAD_DIM).reshape(1, -1),
                     ssd_norm.reshape(1, -1), w_br_ssd.astype(BF16), t["tail_tm"])

    qn = jnp.tile(q_norm, LANES // ATT_HEAD_DIM).reshape(1, -1)
    kn = jnp.tile(k_norm, LANES // ATT_HEAD_DIM).reshape(1, -1)
    qt = _q_prep(proj, cos, sin, qn, bd, perm, t["attn_tq"])
    k = _k_prep(proj, cos, sin, kn, bd, perm, t["prep_tm"])
    vt = _v_t(proj, t["attn_tk"])
    att = _attention(qt, k, vt, t["attn_tq"])

    b_pool = _pool(proj, w_pool.astype(BF16), pool_scale.reshape(1, -1), w_br_pool.astype(BF16), t["pool_tm"])
    return _merge(b_ssd, att, b_pool, proj, gate_bias, x, w_br_att.astype(BF16), w_out.astype(BF16), t["tail_tm"])


def _moe_layer(x, t, upper, ltri, layer, norm_ffn, w_rg, b_rg, w_re, b_re, w_gate_e, w_up_e, w_down_e):
    s = x.shape[0]
    rows_per_tile = t["moe_rows"]
    wt = jnp.zeros((LANES, D_MODEL), F32)
    wt = wt.at[ROUTER_GROUP_ROW:ROUTER_GROUP_ROW + MOE_GROUPS].set(w_rg.T)
    wt = wt.at[ROUTER_EXPERT_ROW:ROUTER_EXPERT_ROW + N_EXPERTS].set(w_re.T)
    bias = jnp.zeros((LANES, 1), F32)
    bias = bias.at[ROUTER_GROUP_ROW:ROUTER_GROUP_ROW + MOE_GROUPS, 0].set(b_rg)
    bias = bias.at[ROUTER_EXPERT_ROW:ROUTER_EXPERT_ROW + N_EXPERTS, 0].set(b_re)

    h, eidx, rank, wts, off, tiles = _router(x, norm_ffn.reshape(1, -1), wt, bias, upper, ltri,
                                             t["router_tm"], rows_per_tile)
    pos = _positions(eidx, rank, off).reshape(-1)
    sorted_rows = 2 * s + N_EXPERTS * rows_per_tile
    xs = _dispatch(pos, h, jnp.zeros((sorted_rows, D_MODEL), F32), t["dispatch_tile"])
    ys = _expert_ffn(tiles[0], tiles[1, 0:1], xs, w_gate_e, w_up_e, w_down_e, layer, rows_per_tile)
    return _combine(pos, ys, x, wts.T, t["combine_tile"])


def kernel(x, norm_mix, w_in, conv_w, conv_b, dt_bias, a_log, d_skip, ssd_norm, w_br_ssd, q_norm, k_norm, w_br_att, w_pool, pool_scale, w_br_pool, gate_bias, w_out, norm_ffn, w_router_group, b_router_group, w_router_expert, b_router_expert, w_gate_e, w_up_e, w_down_e):
    b, s, _ = x.shape
    assert b == 1 and s % 1024 == 0 and s % GRID_W == 0
    t = _tiles(s)
    assert 2 * s // t["moe_rows"] + N_EXPERTS <= TILE_TABLE_LANES
    cos, sin = _rope_tables(s)
    bd, perm = _head_constants()
    e_f, e_b = _expand_matrix(0), _expand_matrix(SSD_HEADS)
    rt = t["router_tm"]
    upper = (jnp.arange(rt)[:, None] < jnp.arange(rt)[None, :]).astype(BF16)
    ltri = (jnp.arange(N_EXPERTS)[None, :] < jnp.arange(N_EXPERTS)[:, None]).astype(BF16)

    xx = x[0]
    for l in range(DEPTH):
        xx = _mixer_layer(xx, t, cos, sin, bd, perm, e_f, e_b, norm_mix[l], w_in[l], conv_w[l], conv_b[l], dt_bias[l],
                          a_log[l], d_skip[l], ssd_norm[l], w_br_ssd[l], q_norm[l], k_norm[l], w_br_att[l],
                          w_pool[l], pool_scale[l], w_br_pool[l], gate_bias[l], w_out[l])
        xx = _moe_layer(xx, t, upper, ltri, l, norm_ffn[l], w_router_group[l], b_router_group[l],
                        w_router_expert[l], b_router_expert[l], w_gate_e, w_up_e, w_down_e)
    return xx[None]
```

```python
import functools
import math

import jax
import jax.numpy as jnp
from jax import lax
from jax.experimental import pallas as pl
from jax.experimental.pallas import tpu as pltpu

F32 = jnp.float32
BF16 = jnp.bfloat16
I32 = jnp.int32
HIGHEST = lax.Precision.HIGHEST

EPS = 1e-6
D_MODEL = 1024
DEPTH = 2

SSD_INNER = 2048
SSD_HEAD_DIM = 64
SSD_HEADS = 32
SSD_GROUPS = 4
SSD_STATE = 128
SSD_CONV = 5
SSD_CHUNK = 128
SSD_BC = SSD_GROUPS * SSD_STATE
SSD_CONV_DIM = SSD_INNER + 2 * SSD_BC
HEADS_PER_GROUP = SSD_HEADS // SSD_GROUPS
GROUP_WIDTH = HEADS_PER_GROUP * SSD_HEAD_DIM

ATT_HEADS = 16
ATT_KV_HEADS = 4
ATT_HEAD_DIM = 64
ATT_WIDTH = 1024
ATT_KV_WIDTH = 256
ATT_REP = ATT_HEADS // ATT_KV_HEADS
ROPE_THETA = 10000.0
GRID_W = 64

POOL_WIDTH = 1024
POOL_WINDOWS = (2, 4, 8, 16)
POOL_GROUP_DIM = 256

N_BRANCH = 3
MOE_GROUPS = 4
MOE_PER_GROUP = 8
N_EXPERTS = 32
EXPERT_FF = 512

LANES = 128
SUBLANES = 8
VMEM_LIMIT_BYTES = 56 * 1024 * 1024

COL_Z = 0
COL_XBC = COL_Z + SSD_INNER
COL_GATES = COL_XBC + SSD_CONV_DIM
COL_Q = COL_GATES + N_BRANCH * D_MODEL
COL_POOL = COL_Q + ATT_WIDTH
COL_K = COL_POOL + POOL_WIDTH
COL_V = COL_K + ATT_KV_WIDTH
PROJ_DIM = COL_V + ATT_KV_WIDTH
DT_PAD = LANES

HALO = 2 * SUBLANES


def _params(*sem):
    return pltpu.CompilerParams(dimension_semantics=sem, vmem_limit_bytes=VMEM_LIMIT_BYTES)


def _silu(v):
    return v * jax.nn.sigmoid(v)


def _split_bf16(v):
    hi = v.astype(BF16)
    lo = (v - hi.astype(F32)).astype(BF16)
    return hi, lo


def _inproj_kernel(x_ref, g_ref, w_ref, wdt_ref, o_ref, dt_ref, h_ref):
    @pl.when(pl.program_id(1) == 0)
    def _():
        x = x_ref[...]
        ms = jnp.mean(x * x, axis=-1, keepdims=True)
        h = (x * lax.rsqrt(ms + EPS) * g_ref[...]).astype(BF16)
        h_ref[...] = h
        dt_ref[...] = jnp.dot(h, wdt_ref[...], preferred_element_type=F32)

    o_ref[...] = jnp.dot(h_ref[...], w_ref[...], preferred_element_type=F32).astype(o_ref.dtype)


def _in_proj(x, g, w, wdt, tm, tn):
    s = x.shape[0]
    return pl.pallas_call(
        _inproj_kernel,
        grid=(s // tm, PROJ_DIM // tn),
        in_specs=[
            pl.BlockSpec((tm, D_MODEL), lambda i, j: (i, 0)),
            pl.BlockSpec((1, D_MODEL), lambda i, j: (0, 0)),
            pl.BlockSpec((D_MODEL, tn), lambda i, j: (0, j)),
            pl.BlockSpec((D_MODEL, DT_PAD), lambda i, j: (0, 0)),
        ],
        out_specs=[
            pl.BlockSpec((tm, tn), lambda i, j: (i, j)),
            pl.BlockSpec((tm, DT_PAD), lambda i, j: (i, 0)),
        ],
        out_shape=[jax.ShapeDtypeStruct((s, PROJ_DIM), BF16), jax.ShapeDtypeStruct((s, DT_PAD), F32)],
        scratch_shapes=[pltpu.VMEM((tm, D_MODEL), BF16)],
        compiler_params=_params("parallel", "arbitrary"),
        name="in_proj",
    )(x, g, w, wdt)


def _fill_ext(ext_ref, main_ref, prev_ref, next_ref, tm):
    i = pl.program_id(0)
    last = pl.num_programs(0) - 1
    ext_ref[0:HALO, :] = jnp.where(i > 0, prev_ref[...].astype(F32), 0.0)
    ext_ref[HALO:HALO + tm, :] = main_ref[...].astype(F32)
    ext_ref[HALO + tm:, :] = jnp.where(i < last, next_ref[...].astype(F32), 0.0)


def _conv_kernel(xm_ref, xp_ref, xn_ref, w_ref, b_ref, o_ref, ext_ref):
    tm = xm_ref.shape[0]
    _fill_ext(ext_ref, xm_ref, xp_ref, xn_ref, tm)
    pad = SSD_CONV // 2
    acc = jnp.broadcast_to(b_ref[...], o_ref.shape)
    for k in range(SSD_CONV):
        acc = acc + ext_ref[pl.ds(HALO - pad + k, tm), :] * w_ref[k:k + 1, :]
    o_ref[...] = _silu(acc).astype(o_ref.dtype)


def _halo_specs(tm, tc, col0, s):
    rb = tm // HALO
    nrb = s // HALO
    return [
        pl.BlockSpec((tm, tc), lambda i, j=0: (i, col0 + j)),
        pl.BlockSpec((HALO, tc), lambda i, j=0: (jnp.maximum(i * rb - 1, 0), col0 + j)),
        pl.BlockSpec((HALO, tc), lambda i, j=0: (jnp.minimum((i + 1) * rb, nrb - 1), col0 + j)),
    ]


def _conv(proj, w, b, tm):
    s = proj.shape[0]
    tc = 1024
    return pl.pallas_call(
        _conv_kernel,
        grid=(s // tm, SSD_CONV_DIM // tc),
        in_specs=_halo_specs(tm, tc, COL_XBC // tc, s) + [
            pl.BlockSpec((SSD_CONV, tc), lambda i, j: (0, j)),
            pl.BlockSpec((1, tc), lambda i, j: (0, j)),
        ],
        out_specs=pl.BlockSpec((tm, tc), lambda i, j: (i, j)),
        out_shape=jax.ShapeDtypeStruct((s, SSD_CONV_DIM), BF16),
        scratch_shapes=[pltpu.VMEM((tm + 2 * HALO, tc), F32)],
        compiler_params=_params("parallel", "parallel"),
        name="conv",
    )(proj, proj, proj, w, b)


def _ssd_direction(x_ref, b_ref, c_ref, dt_ref, e_ref, y_ref, st_ref, bias, a, fwd):
    q = SSD_CHUNK
    row = lax.broadcasted_iota(I32, (q, q), 0)
    col = lax.broadcasted_iota(I32, (q, q), 1)
    mask = (col <= row) if fwd else (col >= row)
    lane0 = 0 if fwd else SSD_HEADS

    dt = jax.nn.softplus(dt_ref[...] + bias)
    da = dt * a
    cs = jnp.dot(mask.astype(F32), da, precision=HIGHEST, preferred_element_type=F32)
    cs_t = cs.T
    edge = q - 1 if fwd else 0
    tot = cs[edge:edge + 1, :]
    dt_hi, dt_lo = _split_bf16(dt)
    dec = jnp.exp(tot)
    dec_hi = dec.astype(BF16).astype(F32)
    r16 = lax.broadcasted_iota(I32, (2 * SUBLANES, LANES), 0)
    dec_rows = jnp.where(r16 == 0, dec_hi, jnp.where(r16 == 1, dec - dec_hi, 0.0)).astype(BF16)
    lhs = jnp.concatenate(
        [dt_hi, dt_lo, jnp.exp(cs).astype(BF16), jnp.exp(tot - cs).astype(BF16), dec_rows], axis=0)
    ex = jnp.dot(lhs, e_ref[...], preferred_element_type=F32)
    dt_x = ex[0:q] + ex[q:2 * q]
    ecs_x = ex[2 * q:3 * q]
    dte_x = ex[3 * q:4 * q]
    dec_x = ex[4 * q:4 * q + 1] + ex[4 * q + 1:4 * q + 2]

    xdt = x_ref[...].astype(F32) * dt_x
    xdte_b = (xdt * dte_x).astype(BF16)
    lane = lax.broadcasted_iota(I32, (q, LANES), 1)
    first_half = lane < SSD_HEAD_DIM
    bm = b_ref[...].astype(BF16)
    cm = c_ref[...].astype(BF16)
    for g in range(SSD_GROUPS):
        bg = bm[:, g * SSD_STATE:(g + 1) * SSD_STATE]
        cg = cm[:, g * SSD_STATE:(g + 1) * SSD_STATE]
        cb = lax.dot_general(cg, bg, (((1,), (1,)), ((), ())), preferred_element_type=F32)
        gs = slice(g * GROUP_WIDTH, (g + 1) * GROUP_WIDTH)
        st = st_ref[g]
        y_off = jnp.dot(cg, st.astype(BF16), preferred_element_type=F32) * ecs_x[:, gs]
        for pr in range(HEADS_PER_GROUP // 2):
            c0 = g * GROUP_WIDTH + pr * LANES
            xp = xdt[:, c0:c0 + LANES]
            halves = (jnp.where(first_half, xp, 0.0).astype(BF16), jnp.where(first_half, 0.0, xp).astype(BF16))
            yd = None
            for k in range(2):
                hl = lane0 + g * HEADS_PER_GROUP + 2 * pr + k
                seg = cs[:, hl:hl + 1] - cs_t[hl:hl + 1, :]
                decay = jnp.exp(jnp.where(mask, seg, -jnp.inf))
                part = jnp.dot((cb * decay).astype(BF16), halves[k], preferred_element_type=F32)
                yd = part if yd is None else yd + part
            y_ref[:, c0:c0 + LANES] = (yd + y_off[:, pr * LANES:(pr + 1) * LANES]).astype(y_ref.dtype)
        upd = lax.dot_general(bg, xdte_b[:, gs], (((0,), (0,)), ((), ())), preferred_element_type=F32)
        st_ref[g] = st * dec_x[:, gs] + upd


def _ssd_kernel(xf_ref, bf_ref, cf_ref, dtf_ref, xb_ref, bb_ref, cb_ref, dtb_ref,
                bias_ref, alog_ref, ef_ref, eb_ref, yf_ref, yb_ref, stf_ref, stb_ref):
    @pl.when(pl.program_id(0) == 0)
    def _():
        stf_ref[...] = jnp.zeros_like(stf_ref)
        stb_ref[...] = jnp.zeros_like(stb_ref)

    bias = bias_ref[...]
    a = -jnp.exp(alog_ref[...])
    _ssd_direction(xf_ref, bf_ref, cf_ref, dtf_ref, ef_ref, yf_ref, stf_ref, bias, a, True)
    _ssd_direction(xb_ref, bb_ref, cb_ref, dtb_ref, eb_ref, yb_ref, stb_ref, bias, a, False)


def _ssd(xbc, dt_raw, dt_bias, a_log, e_f, e_b):
    s = xbc.shape[0]
    q = SSD_CHUNK
    nc = s // q
    nb = SSD_INNER // SSD_BC

    def chunk_specs(im):
        return [
            pl.BlockSpec((q, SSD_INNER), lambda i: (im(i), 0)),
            pl.BlockSpec((q, SSD_BC), lambda i: (im(i), nb)),
            pl.BlockSpec((q, SSD_BC), lambda i: (im(i), nb + 1)),
            pl.BlockSpec((q, DT_PAD), lambda i: (im(i), 0)),
        ]

    fwd = lambda i: i
    bwd = lambda i: nc - 1 - i
    const = lambda shape: pl.BlockSpec(shape, lambda i: (0,) * len(shape))
    st_shape = (SSD_GROUPS, SSD_STATE, GROUP_WIDTH)
    return pl.pallas_call(
        _ssd_kernel,
        grid=(nc,),
        in_specs=chunk_specs(fwd) + chunk_specs(bwd) + [
            const((1, DT_PAD)), const((1, DT_PAD)), const((LANES, SSD_INNER)), const((LANES, SSD_INNER))],
        out_specs=[pl.BlockSpec((q, SSD_INNER), lambda i: (fwd(i), 0)),
                   pl.BlockSpec((q, SSD_INNER), lambda i: (bwd(i), 0))],
        out_shape=[jax.ShapeDtypeStruct((s, SSD_INNER), BF16)] * 2,
        scratch_shapes=[pltpu.VMEM(st_shape, F32), pltpu.VMEM(st_shape, F32)],
        compiler_params=_params("arbitrary"),
        name="ssd",
    )(xbc, xbc, xbc, dt_raw, xbc, xbc, xbc, dt_raw, dt_bias, a_log, e_f, e_b)


def _norm_rope(x, cos, sin, g, bd, perm):
    hi, lo = _split_bf16(x * x)
    ss = jnp.dot(hi, bd, preferred_element_type=F32) + jnp.dot(lo, bd, preferred_element_type=F32)
    y = x * lax.rsqrt(ss * (1.0 / ATT_HEAD_DIM) + EPS) * g
    yh, yl = _split_bf16(y)
    rot = jnp.dot(yh, perm, preferred_element_type=F32) + jnp.dot(yl, perm, preferred_element_type=F32)
    return y * cos + rot * sin


def _qprep_kernel(x_ref, cos_ref, sin_ref, g_ref, bd_ref, perm_ref, o_ref, *, scale):
    tq = x_ref.shape[0]
    cos, sin, g, bd, perm = cos_ref[...], sin_ref[...], g_ref[...], bd_ref[...], perm_ref[...]
    for j in range(ATT_WIDTH // LANES):
        out = _norm_rope(x_ref[:, j * LANES:(j + 1) * LANES].astype(F32), cos, sin, g, bd, perm) * scale
        out_t = out.T.astype(BF16)
        grp, r0 = (2 * j) // ATT_REP, (2 * j) % ATT_REP
        o_ref[grp, :, r0 * tq:(r0 + 1) * tq] = out_t[:ATT_HEAD_DIM]
        o_ref[grp, :, (r0 + 1) * tq:(r0 + 2) * tq] = out_t[ATT_HEAD_DIM:]


def _kprep_kernel(x_ref, cos_ref, sin_ref, g_ref, bd_ref, perm_ref, o_ref):
    out = _norm_rope(x_ref[...].astype(F32), cos_ref[...], sin_ref[...], g_ref[...], bd_ref[...],
                     perm_ref[...]).astype(BF16)
    o_ref[0] = out[:, :ATT_HEAD_DIM]
    o_ref[1] = out[:, ATT_HEAD_DIM:]


VT_ROWS = ATT_HEAD_DIM + 2 * SUBLANES


def _vt_kernel(x_ref, o_ref):
    xt = x_ref[...].astype(F32).T.astype(BF16)
    pad = VT_ROWS - ATT_HEAD_DIM
    ones_row = (lax.broadcasted_iota(I32, (pad, xt.shape[1]), 0) == 0).astype(BF16)
    for g in range(ATT_KV_HEADS):
        o_ref[g * VT_ROWS:g * VT_ROWS + ATT_HEAD_DIM, :] = xt[g * ATT_HEAD_DIM:(g + 1) * ATT_HEAD_DIM]
        o_ref[g * VT_ROWS + ATT_HEAD_DIM:(g + 1) * VT_ROWS, :] = ones_row


def _rope_specs(tm):
    return [
        pl.BlockSpec((tm, LANES), lambda i, j: (i, 0)),
        pl.BlockSpec((tm, LANES), lambda i, j: (i, 0)),
        pl.BlockSpec((1, LANES), lambda i, j: (0, 0)),
        pl.BlockSpec((LANES, LANES), lambda i, j: (0, 0)),
        pl.BlockSpec((LANES, LANES), lambda i, j: (0, 0)),
    ]


def _q_prep(proj, cos, sin, g, bd, perm, tq):
    s = proj.shape[0]
    scale = ATT_HEAD_DIM ** -0.5 * math.log2(math.e)
    const = lambda i: (0, 0)
    return pl.pallas_call(
        functools.partial(_qprep_kernel, scale=scale),
        grid=(s // tq,),
        in_specs=[
            pl.BlockSpec((tq, ATT_WIDTH), lambda i: (i, COL_Q // ATT_WIDTH)),
            pl.BlockSpec((tq, LANES), lambda i: (i, 0)),
            pl.BlockSpec((tq, LANES), lambda i: (i, 0)),
            pl.BlockSpec((1, LANES), const),
            pl.BlockSpec((LANES, LANES), const),
            pl.BlockSpec((LANES, LANES), const),
        ],
        out_specs=pl.BlockSpec((ATT_KV_HEADS, ATT_HEAD_DIM, ATT_REP * tq), lambda i: (0, 0, i)),
        out_shape=jax.ShapeDtypeStruct((ATT_KV_HEADS, ATT_HEAD_DIM, ATT_REP * s), BF16),
        compiler_params=_params("parallel"),
        name="q_prep",
    )(proj, cos, sin, g, bd, perm)


def _k_prep(proj, cos, sin, g, bd, perm, tm):
    s = proj.shape[0]
    return pl.pallas_call(
        _kprep_kernel,
        grid=(s // tm, ATT_KV_WIDTH // LANES),
        in_specs=[pl.BlockSpec((tm, LANES), lambda i, j: (i, COL_K // LANES + j))] + _rope_specs(tm),
        out_specs=pl.BlockSpec((2, tm, ATT_HEAD_DIM), lambda i, j: (j, i, 0)),
        out_shape=jax.ShapeDtypeStruct((ATT_KV_HEADS, s, ATT_HEAD_DIM), BF16),
        compiler_params=_params("parallel", "parallel"),
        name="k_prep",
    )(proj, cos, sin, g, bd, perm)


def _v_t(proj, tk):
    s = proj.shape[0]
    rows = ATT_KV_HEADS * VT_ROWS
    return pl.pallas_call(
        _vt_kernel,
        grid=(s // tk,),
        in_specs=[pl.BlockSpec((tk, ATT_KV_WIDTH), lambda i: (i, COL_V // ATT_KV_WIDTH))],
        out_specs=pl.BlockSpec((None, rows, tk), lambda i: (i, 0, 0)),
        out_shape=jax.ShapeDtypeStruct((s // tk, rows, tk), BF16),
        compiler_params=_params("parallel"),
        name="v_t",
    )(proj)


SAFE_SCORE_SPAN = 100.0


FAST_SWEEP_UNROLL = 4


def _attn_kernel(qt_ref, k_ref, vt_ref, o_ref, kmax_ref, shift_ref, acc_ref, pa_ref, pb_ref, sa_ref, sb_ref, *, tk):
    s = k_ref.shape[0]
    nq = qt_ref.shape[1]
    n_chunks = s // tk

    @pl.when(pl.program_id(1) == 0)
    def _():
        def key_norm(c, best):
            kc = k_ref[pl.ds(pl.multiple_of(c * tk, tk), tk), :].astype(F32)
            norm = jnp.sum(kc * kc, axis=1, keepdims=True)
            return jnp.maximum(best, jnp.max(norm, axis=0, keepdims=True))

        best = lax.fori_loop(0, n_chunks, key_norm, jnp.zeros((1, 1), F32))
        kmax_ref[...] = jnp.broadcast_to(best, kmax_ref.shape)

    q = qt_ref[...].astype(F32)
    bound = jnp.sqrt(jnp.sum(q * q, axis=0, keepdims=True) * kmax_ref[0:1, 0:1])
    safe = 2.0 * jnp.max(bound) <= SAFE_SCORE_SPAN
    acc_ref[...] = jnp.zeros_like(acc_ref)

    def scores(c):
        kc = k_ref[pl.ds(pl.multiple_of(c * tk, tk), tk), :]
        return jnp.dot(kc, qt_ref[...], preferred_element_type=F32)

    def sweep(produce, consume, buf_a, buf_b, unroll):
        produce(0, buf_a)

        def body(i, carry):
            c = 2 * i
            produce(c + 1, buf_b)
            consume(c, buf_a)
            produce(jnp.minimum(c + 2, n_chunks - 1), buf_a)
            consume(c + 1, buf_b)
            return carry

        lax.fori_loop(0, n_chunks // 2, body, 0, unroll=unroll)

    @pl.when(safe)
    def _():
        def produce(c, dst_ref):
            dst_ref[...] = jnp.exp2(scores(c) - bound).astype(BF16)

        def consume(c, src_ref):
            acc_ref[...] += jnp.dot(vt_ref[c], src_ref[...], preferred_element_type=F32)

        sweep(produce, consume, pa_ref, pb_ref, unroll=FAST_SWEEP_UNROLL)

    @pl.when(jnp.logical_not(safe))
    def _():
        shift_ref[...] = jnp.full_like(shift_ref, -jnp.inf)

        def produce(c, dst_ref):
            dst_ref[...] = scores(c)

        def consume(c, src_ref):
            st = src_ref[...]
            m_old = shift_ref[...]
            m_new = jnp.maximum(m_old, jnp.max(st, axis=0, keepdims=True))
            p = jnp.exp2(st - m_new).astype(BF16)
            acc_ref[...] = jnp.exp2(m_old - m_new) * acc_ref[...] + jnp.dot(vt_ref[c], p, preferred_element_type=F32)
            shift_ref[...] = m_new

        sweep(produce, consume, sa_ref, sb_ref, unroll=1)

    acc = acc_ref[...]
    out = acc[:ATT_HEAD_DIM] / acc[ATT_HEAD_DIM:ATT_HEAD_DIM + 1]
    tq = nq // ATT_REP
    heads = [out[:, r * tq:(r + 1) * tq] for r in range(ATT_REP)]
    o_ref[...] = jnp.concatenate(heads, axis=0).T.astype(o_ref.dtype)


def _attention(qt, k, vt, tq):
    s = k.shape[1]
    tk = vt.shape[2]
    nq = ATT_REP * tq
    return pl.pallas_call(
        functools.partial(_attn_kernel, tk=tk),
        grid=(ATT_KV_HEADS, s // tq),
        in_specs=[
            pl.BlockSpec((None, ATT_HEAD_DIM, nq), lambda g, i: (g, 0, i)),
            pl.BlockSpec((None, s, ATT_HEAD_DIM), lambda g, i: (g, 0, 0)),
            pl.BlockSpec((s // tk, VT_ROWS, tk), lambda g, i: (0, g, 0)),
        ],
        out_specs=pl.BlockSpec((tq, ATT_REP * ATT_HEAD_DIM), lambda g, i: (i, g)),
        out_shape=jax.ShapeDtypeStruct((s, ATT_WIDTH), BF16),
        scratch_shapes=[pltpu.VMEM((SUBLANES, LANES), F32), pltpu.VMEM((1, nq), F32), pltpu.VMEM((VT_ROWS, nq), F32),
                        pltpu.VMEM((tk, nq), BF16), pltpu.VMEM((tk, nq), BF16),
                        pltpu.VMEM((tk, nq), F32), pltpu.VMEM((tk, nq), F32)],
        compiler_params=_params("parallel", "arbitrary"),
        name="attention",
    )(qt, k, vt)


def _ssd_out_kernel(yf_ref, yb_ref, xs_ref, z_ref, dskip_ref, g_ref, w_ref, o_ref):
    y = yf_ref[...].astype(F32) + yb_ref[...].astype(F32) + xs_ref[...].astype(F32) * dskip_ref[...]
    y = y * _silu(z_ref[...].astype(F32))
    ms = jnp.mean(y * y, axis=-1, keepdims=True)
    yn = (y * lax.rsqrt(ms + EPS) * g_ref[...]).astype(BF16)
    o_ref[...] = jnp.dot(yn, w_ref[...], preferred_element_type=F32).astype(o_ref.dtype)


def _ssd_out(yf, yb, xbc, proj, dskip, g, w, tm):
    s = yf.shape[0]
    row = lambda i: (i, 0)
    const = lambda i: (0, 0)
    return pl.pallas_call(
        _ssd_out_kernel,
        grid=(s // tm,),
        in_specs=[
            pl.BlockSpec((tm, SSD_INNER), row), pl.BlockSpec((tm, SSD_INNER), row),
            pl.BlockSpec((tm, SSD_INNER), row), pl.BlockSpec((tm, SSD_INNER), row),
            pl.BlockSpec((1, SSD_INNER), const), pl.BlockSpec((1, SSD_INNER), const),
            pl.BlockSpec((SSD_INNER, D_MODEL), const),
        ],
        out_specs=pl.BlockSpec((tm, D_MODEL), row),
        out_shape=jax.ShapeDtypeStruct((s, D_MODEL), BF16),
        compiler_params=_params("parallel"),
        name="ssd_out",
    )(yf, yb, xbc, proj, dskip, g, w)


def _pool_kernel(um_ref, up_ref, un_ref, wp_ref, scale_ref, w_ref, o_ref, ext_ref, *, seq):
    tm = um_ref.shape[0]
    _fill_ext(ext_ref, um_ref, up_ref, un_ref, tm)
    t = pl.program_id(0) * tm + lax.broadcasted_iota(I32, (tm, 1), 0)
    mixed = []
    for gi, win in enumerate(POOL_WINDOWS):
        half = win // 2
        cols = slice(gi * POOL_GROUP_DIM, (gi + 1) * POOL_GROUP_DIM)
        acc = ext_ref[pl.ds(HALO - half, tm), cols]
        for k in range(1, win):
            acc = acc + ext_ref[pl.ds(HALO - half + k, tm), cols]
        cnt = (jnp.minimum(t + half, seq) - jnp.maximum(t - half, 0)).astype(F32)
        mean = acc / cnt
        mix = (mean - ext_ref[pl.ds(HALO, tm), cols]).astype(BF16)
        mixed.append(jnp.dot(mix, wp_ref[gi], preferred_element_type=F32))
    pooled = (jnp.concatenate(mixed, axis=1) * scale_ref[...]).astype(BF16)
    o_ref[...] = jnp.dot(pooled, w_ref[...], preferred_element_type=F32).astype(o_ref.dtype)


def _pool(proj, w_pool, scale, w_br, tm):
    s = proj.shape[0]
    return pl.pallas_call(
        functools.partial(_pool_kernel, seq=s),
        grid=(s // tm,),
        in_specs=_halo_specs(tm, POOL_WIDTH, COL_POOL // POOL_WIDTH, s) + [
            pl.BlockSpec((len(POOL_WINDOWS), POOL_GROUP_DIM, POOL_GROUP_DIM), lambda i: (0, 0, 0)),
            pl.BlockSpec((1, POOL_WIDTH), lambda i: (0, 0)),
            pl.BlockSpec((POOL_WIDTH, D_MODEL), lambda i: (0, 0)),
        ],
        out_specs=pl.BlockSpec((tm, D_MODEL), lambda i: (i, 0)),
        out_shape=jax.ShapeDtypeStruct((s, D_MODEL), BF16),
        scratch_shapes=[pltpu.VMEM((tm + 2 * HALO, POOL_WIDTH), F32)],
        compiler_params=_params("parallel"),
        name="pool",
    )(proj, proj, proj, w_pool, scale, w_br)


def _merge_kernel(bs_ref, att_ref, bp_ref, g0_ref, g1_ref, g2_ref, gb_ref, x_ref, watt_ref, wout_ref, o_ref):
    b_att = jnp.dot(att_ref[...], watt_ref[...], preferred_element_type=F32)
    gb = gb_ref[...]
    merged = (jax.nn.sigmoid(g0_ref[...] + gb[0:1]) * bs_ref[...]
              + jax.nn.sigmoid(g1_ref[...] + gb[1:2]) * b_att
              + jax.nn.sigmoid(g2_ref[...] + gb[2:3]) * bp_ref[...])
    o_ref[...] = x_ref[...] + jnp.dot(merged.astype(BF16), wout_ref[...], preferred_element_type=F32)


def _merge(b_ssd, att, b_pool, proj, gate_bias, x, w_att, w_out, tm):
    s = x.shape[0]
    row = lambda i: (i, 0)
    const = lambda i: (0, 0)
    gcol = COL_GATES // D_MODEL
    tile = pl.BlockSpec((tm, D_MODEL), row)
    return pl.pallas_call(
        _merge_kernel,
        grid=(s // tm,),
        in_specs=[
            tile, tile, tile,
            pl.BlockSpec((tm, D_MODEL), lambda i: (i, gcol)),
            pl.BlockSpec((tm, D_MODEL), lambda i: (i, gcol + 1)),
            pl.BlockSpec((tm, D_MODEL), lambda i: (i, gcol + 2)),
            pl.BlockSpec((N_BRANCH, D_MODEL), const),
            tile,
            pl.BlockSpec((ATT_WIDTH, D_MODEL), const),
            pl.BlockSpec((D_MODEL, D_MODEL), const),
        ],
        out_specs=tile,
        out_shape=jax.ShapeDtypeStruct((s, D_MODEL), F32),
        compiler_params=_params("parallel"),
        name="merge",
    )(b_ssd, att, b_pool, proj, proj, proj, gate_bias, x, w_att, w_out)


ROUTER_GROUP_ROW = 0
ROUTER_EXPERT_ROW = SUBLANES
TILE_TABLE_LANES = 256


def _first_argmax(v, idx, big):
    m = jnp.max(v, axis=0, keepdims=True)
    return m, jnp.min(jnp.where(v == m, idx, big), axis=0, keepdims=True)


def _router_kernel(x_ref, g_ref, wt_ref, b_ref, upper_ref, ltri_ref,
                   h_ref, eidx_ref, rank_ref, wts_ref, off_ref, tiles_ref, count_ref, *, tile_rows):
    i = pl.program_id(0)
    tm = x_ref.shape[0]

    @pl.when(i == 0)
    def _():
        count_ref[...] = jnp.zeros_like(count_ref)

    x = x_ref[...]
    ms = jnp.mean(x * x, axis=-1, keepdims=True)
    h = x * lax.rsqrt(ms + EPS) * g_ref[...]
    h_ref[...] = h
    lt = lax.dot_general(wt_ref[...], h, (((1,), (1,)), ((), ())), precision=HIGHEST,
                         preferred_element_type=F32) + b_ref[...]

    r8 = lax.broadcasted_iota(I32, (SUBLANES, tm), 0)
    gl = jnp.where(r8 < MOE_GROUPS, lt[ROUTER_GROUP_ROW:ROUTER_GROUP_ROW + SUBLANES], -jnp.inf)
    gmax, gsel = _first_argmax(gl, r8, SUBLANES)
    p_group = 1.0 / jnp.sum(jnp.exp(gl - gmax), axis=0, keepdims=True)

    r32 = lax.broadcasted_iota(I32, (N_EXPERTS, tm), 0)
    el = lt[ROUTER_EXPERT_ROW:ROUTER_EXPERT_ROW + N_EXPERTS]
    el = jnp.where(lax.shift_right_logical(r32, int(math.log2(MOE_PER_GROUP))) == gsel, el, -jnp.inf)
    m1, i1 = _first_argmax(el, r32, N_EXPERTS)
    el2 = jnp.where(r32 == i1, -jnp.inf, el)
    m2, i2 = _first_argmax(el2, r32, N_EXPERTS)
    z = jnp.sum(jnp.exp(el - m1), axis=0, keepdims=True)
    p1 = 1.0 / z
    p2 = jnp.exp(m2 - m1) / z
    psum = p1 + p2
    wts_ref[0:1, :] = p_group * (p1 / psum)
    wts_ref[1:2, :] = p_group * (p2 / psum)
    eidx_ref[0:1, :] = i1
    eidx_ref[1:2, :] = i2

    sel0 = r32 == i1
    sel1 = r32 == i2
    member = jnp.where(sel0 | sel1, 1.0, 0.0)
    before = jnp.dot(member.astype(BF16), upper_ref[...], preferred_element_type=F32) + count_ref[:, 0:1]
    rank_ref[0:1, :] = jnp.sum(jnp.where(sel0, before, 0.0), axis=0, keepdims=True).astype(I32)
    rank_ref[1:2, :] = jnp.sum(jnp.where(sel1, before, 0.0), axis=0, keepdims=True).astype(I32)
    count_ref[...] = count_ref[...] + jnp.sum(member, axis=1, keepdims=True)

    @pl.when(i == pl.num_programs(0) - 1)
    def _():
        cnt = count_ref[...]
        ntile = jnp.floor((cnt + (tile_rows - 1)) * (1.0 / tile_rows))
        start = jnp.dot(ltri_ref[...], ntile.astype(BF16), preferred_element_type=F32)
        off_ref[...] = (start * tile_rows).astype(I32)
        end = (start + ntile)[:, 0:1]
        tid = lax.broadcasted_iota(I32, (N_EXPERTS, TILE_TABLE_LANES), 1).astype(F32)
        owner = jnp.sum(jnp.where(end <= tid, 1.0, 0.0), axis=0, keepdims=True)
        owner = jnp.minimum(owner, N_EXPERTS - 1.0)
        nact = jnp.sum(ntile[:, 0:1], axis=0, keepdims=True)
        r = lax.broadcasted_iota(I32, (SUBLANES, TILE_TABLE_LANES), 0)
        tiles_ref[...] = jnp.where(r == 0, owner, nact).astype(I32)


def _router(x, g, wt, b, upper, ltri, tm, tile_rows):
    s = x.shape[0]
    row = lambda i: (i, 0)
    const = lambda i: (0, 0)
    lane = lambda i: (0, i)
    return pl.pallas_call(
        functools.partial(_router_kernel, tile_rows=tile_rows),
        grid=(s // tm,),
        in_specs=[
            pl.BlockSpec((tm, D_MODEL), row),
            pl.BlockSpec((1, D_MODEL), const),
            pl.BlockSpec((LANES, D_MODEL), const),
            pl.BlockSpec((LANES, 1), const),
            pl.BlockSpec((tm, tm), const),
            pl.BlockSpec((N_EXPERTS, N_EXPERTS), const),
        ],
        out_specs=[
            pl.BlockSpec((tm, D_MODEL), row),
            pl.BlockSpec((2, tm), lane),
            pl.BlockSpec((2, tm), lane),
            pl.BlockSpec((2, tm), lane),
            pl.BlockSpec((N_EXPERTS, LANES), const),
            pl.BlockSpec((SUBLANES, TILE_TABLE_LANES), const),
        ],
        out_shape=[
            jax.ShapeDtypeStruct((s, D_MODEL), F32),
            jax.ShapeDtypeStruct((2, s), I32),
            jax.ShapeDtypeStruct((2, s), I32),
            jax.ShapeDtypeStruct((2, s), F32),
            jax.ShapeDtypeStruct((N_EXPERTS, LANES), I32),
            jax.ShapeDtypeStruct((SUBLANES, TILE_TABLE_LANES), I32),
        ],
        scratch_shapes=[pltpu.VMEM((N_EXPERTS, LANES), F32)],
        compiler_params=_params("arbitrary"),
        name="router",
    )(x, g, wt, b, upper, ltri)


def _positions_kernel(eidx_ref, rank_ref, off_ref, pos_ref):
    e = eidx_ref[...]
    pos = rank_ref[...]
    for x in range(N_EXPERTS):
        pos = pos + jnp.where(e == x, off_ref[x:x + 1, 0:1], 0)
    pos_ref[...] = pos


def _positions(eidx, rank, off):
    full = lambda a: pl.BlockSpec(a.shape, lambda i: (0, 0))
    return pl.pallas_call(
        _positions_kernel,
        grid=(1,),
        in_specs=[full(eidx), full(rank), full(off)],
        out_specs=full(eidx),
        out_shape=jax.ShapeDtypeStruct(eidx.shape, I32),
        compiler_params=_params("arbitrary"),
        name="positions",
    )(eidx, rank, off)


ROW_DMA_UNROLL = 8


def _dispatch_kernel(pos_s, h_ref, xs_in_hbm, xs_hbm, sem, *, seq):
    del xs_in_hbm
    tile = h_ref.shape[0]
    base = pl.program_id(0) * tile

    def row_copy(t, slot):
        pos = pos_s[slot * seq + base + t]
        return pltpu.make_async_copy(h_ref.at[pl.ds(t, 1), :], xs_hbm.at[pl.ds(pos, 1), :], sem)

    def start(t, c):
        row_copy(t, 0).start()
        row_copy(t, 1).start()
        return c

    def wait(t, c):
        row_copy(t, 0).wait()
        row_copy(t, 1).wait()
        return c

    lax.fori_loop(0, tile, start, 0, unroll=ROW_DMA_UNROLL)
    lax.fori_loop(0, tile, wait, 0, unroll=ROW_DMA_UNROLL)


def _dispatch(pos, h, xs_zero, tile):
    s = h.shape[0]
    return pl.pallas_call(
        functools.partial(_dispatch_kernel, seq=s),
        grid_spec=pltpu.PrefetchScalarGridSpec(
            num_scalar_prefetch=1,
            grid=(s // tile,),
            in_specs=[pl.BlockSpec((tile, D_MODEL), lambda i, *_: (i, 0)), pl.BlockSpec(memory_space=pl.ANY)],
            out_specs=pl.BlockSpec(memory_space=pl.ANY),
            scratch_shapes=[pltpu.SemaphoreType.DMA],
        ),
        out_shape=jax.ShapeDtypeStruct(xs_zero.shape, xs_zero.dtype),
        input_output_aliases={2: 0},
        compiler_params=_params("arbitrary"),
        name="dispatch",
    )(pos, h, xs_zero)


def _ffn_kernel(owner_s, nact_s, x_ref, wg_ref, wu_ref, wd_ref, o_ref):
    j = pl.program_id(0)

    @pl.when(j < nact_s[0])
    def _():
        x = x_ref[...].astype(BF16)
        gate = jnp.dot(x, wg_ref[...].astype(BF16), preferred_element_type=F32)
        up = jnp.dot(x, wu_ref[...].astype(BF16), preferred_element_type=F32)
        hid = (_silu(gate) * up).astype(BF16)
        o_ref[...] = jnp.dot(hid, wd_ref[...].astype(BF16), preferred_element_type=F32)

    @pl.when(j >= nact_s[0])
    def _():
        o_ref[...] = jnp.zeros_like(o_ref)


def _expert_ffn(owner, nact, xs, wg, wu, wd, layer, tile_rows):
    rows = xs.shape[0]
    return pl.pallas_call(
        _ffn_kernel,
        grid_spec=pltpu.PrefetchScalarGridSpec(
            num_scalar_prefetch=2,
            grid=(rows // tile_rows,),
            in_specs=[
                pl.BlockSpec((tile_rows, D_MODEL), lambda j, o, n: (j, 0)),
                pl.BlockSpec((None, None, D_MODEL, EXPERT_FF), lambda j, o, n: (layer, o[j], 0, 0)),
                pl.BlockSpec((None, None, D_MODEL, EXPERT_FF), lambda j, o, n: (layer, o[j], 0, 0)),
                pl.BlockSpec((None, None, EXPERT_FF, D_MODEL), lambda j, o, n: (layer, o[j], 0, 0)),
            ],
            out_specs=pl.BlockSpec((tile_rows, D_MODEL), lambda j, o, n: (j, 0)),
        ),
        out_shape=jax.ShapeDtypeStruct((rows, D_MODEL), F32),
        compiler_params=_params("arbitrary"),
        name="expert_ffn",
    )(owner, nact, xs, wg, wu, wd)


def _combine_kernel(pos_s, ys_hbm, x_ref, w_ref, o_ref, buf_ref, sems, *, seq):
    tile = x_ref.shape[0]
    i = pl.program_id(0)

    def row_copy(step, t, slot):
        half = lax.rem(step, 2)
        pos = pos_s[slot * seq + step * tile + t]
        return pltpu.make_async_copy(ys_hbm.at[pl.ds(pos, 1), :], buf_ref.at[half, slot, pl.ds(t, 1), :],
                                     sems.at[half])

    def start_tile(step):
        def start(t, c):
            row_copy(step, t, 0).start()
            row_copy(step, t, 1).start()
            return c

        lax.fori_loop(0, tile, start, 0, unroll=ROW_DMA_UNROLL)

    def wait_tile(step):
        def wait(t, c):
            row_copy(step, t, 0).wait()
            row_copy(step, t, 1).wait()
            return c

        lax.fori_loop(0, tile, wait, 0, unroll=ROW_DMA_UNROLL)

    @pl.when(i == 0)
    def _():
        start_tile(0)

    @pl.when(i + 1 < pl.num_programs(0))
    def _():
        start_tile(i + 1)

    wait_tile(i)
    half = lax.rem(i, 2)
    w = w_ref[...]
    o_ref[...] = x_ref[...] + w[:, 0:1] * buf_ref[half, 0] + w[:, 1:2] * buf_ref[half, 1]


def _combine(pos, ys, x, w_cols, tile):
    s = x.shape[0]
    return pl.pallas_call(
        functools.partial(_combine_kernel, seq=s),
        grid_spec=pltpu.PrefetchScalarGridSpec(
            num_scalar_prefetch=1,
            grid=(s // tile,),
            in_specs=[
                pl.BlockSpec(memory_space=pl.ANY),
                pl.BlockSpec((tile, D_MODEL), lambda i, *_: (i, 0)),
                pl.BlockSpec((tile, 2), lambda i, *_: (i, 0)),
            ],
            out_specs=pl.BlockSpec((tile, D_MODEL), lambda i, *_: (i, 0)),
            scratch_shapes=[pltpu.VMEM((2, 2, tile, D_MODEL), F32), pltpu.SemaphoreType.DMA((2,))],
        ),
        out_shape=jax.ShapeDtypeStruct((s, D_MODEL), F32),
        compiler_params=_params("arbitrary"),
        name="combine",
    )(pos, ys, x, w_cols)


def _tiles(s):
    return dict(
        inproj_tm=min(1024, s), inproj_tn=768,
        conv_tm=min(512, s), prep_tm=min(1024, s),
        attn_tq=min(512, s), attn_tk=256,
        tail_tm=min(256, s), pool_tm=min(512, s),
        router_tm=min(512, s), moe_rows=256, dispatch_tile=min(512, s), combine_tile=min(256, s),
    )


def _rope_tables(s):
    rows = s // GRID_W
    row = jnp.repeat(jnp.arange(rows), GRID_W).astype(F32)
    col = jnp.tile(jnp.arange(GRID_W), rows).astype(F32)
    half = ATT_HEAD_DIM // 2
    inv_freq = ROPE_THETA ** (-jnp.arange(0, half, 2, dtype=F32) / half)
    ang = jnp.concatenate([row[:, None] * inv_freq, col[:, None] * inv_freq], axis=-1)
    d = jnp.arange(LANES)
    idx = ((d % ATT_HEAD_DIM) // half) * (half // 2) + d % (half // 2)
    return jnp.cos(ang)[:, idx], jnp.sin(ang)[:, idx]


def _head_constants():
    d = jnp.arange(LANES)
    same_head = (d[:, None] // ATT_HEAD_DIM) == (d[None, :] // ATT_HEAD_DIM)
    bd = same_head.astype(BF16)
    quarter = ATT_HEAD_DIM // 4
    second = (d % (2 * quarter)) >= quarter
    src = jnp.where(second, d - quarter, d + quarter)
    sign = jnp.where(second, 1.0, -1.0)
    perm = jnp.zeros((LANES, LANES), F32).at[src, d].set(sign).astype(BF16)
    return bd, perm


def _expand_matrix(lane0):
    r = jnp.arange(LANES)[:, None]
    c = jnp.arange(SSD_INNER)[None, :]
    return (r == lane0 + c // SSD_HEAD_DIM).astype(BF16)


def _pad_lanes(v, width=LANES):
    flat = v.reshape(1, -1)
    return jnp.pad(flat, ((0, 0), (0, width - flat.shape[1])))


def _mixer_layer(x, t, cos, sin, bd, perm, e_f, e_b, norm_mix, w_in, conv_w, conv_b, dt_bias, a_log, d_skip,
                 ssd_norm, w_br_ssd, q_norm, k_norm, w_br_att, w_pool, pool_scale, w_br_pool, gate_bias, w_out):
    o_z, o_xbc = 0, SSD_INNER
    o_dt = o_xbc + SSD_CONV_DIM
    o_q = o_dt + 2 * SSD_HEADS
    o_kv = o_q + ATT_WIDTH
    o_pool = o_kv + 2 * ATT_KV_WIDTH
    o_gate = o_pool + POOL_WIDTH
    w_main = jnp.concatenate(
        [w_in[:, o_z:o_dt], w_in[:, o_gate:], w_in[:, o_q:o_kv], w_in[:, o_pool:o_gate], w_in[:, o_kv:o_pool]],
        axis=1).astype(BF16)
    w_dt = jnp.pad(w_in[:, o_dt:o_q], ((0, 0), (0, DT_PAD - 2 * SSD_HEADS))).astype(BF16)

    proj, dt_raw = _in_proj(x, norm_mix.reshape(1, -1), w_main, w_dt, t["inproj_tm"], t["inproj_tn"])

    xbc = _conv(proj, conv_w, conv_b.reshape(1, -1), t["conv_tm"])
    y_f, y_b = _ssd(xbc, dt_raw, _pad_lanes(dt_bias), _pad_lanes(a_log), e_f, e_b)
    b_ssd = _ssd_out(y_f, y_b, xbc, proj, jnp.repeat(d_skip, SSD_HEAD_DIM).reshape(1, -1),
                     ssd_norm.reshape(1, -1), w_br_ssd.astype(BF16), t["tail_tm"])

    qn = jnp.tile(q_norm, LANES // ATT_HEAD_DIM).reshape(1, -1)
    kn = jnp.tile(k_norm, LANES // ATT_HEAD_DIM).reshape(1, -1)
    qt = _q_prep(proj, cos, sin, qn, bd, perm, t["attn_tq"])
    k = _k_prep(proj, cos, sin, kn, bd, perm, t["prep_tm"])
    vt = _v_t(proj, t["attn_tk"])
    att = _attention(qt, k, vt, t["attn_tq"])

    b_pool = _pool(proj, w_pool.astype(BF16), pool_scale.reshape(1, -1), w_br_pool.astype(BF16), t["pool_tm"])
    return _merge(b_ssd, att, b_pool, proj, gate_bias, x, w_br_att.astype(BF16), w_out.astype(BF16), t["tail_tm"])


def _moe_layer(x, t, upper, ltri, layer, norm_ffn, w_rg, b_rg, w_re, b_re, w_gate_e, w_up_e, w_down_e):
    s = x.shape[0]
    rows_per_tile = t["moe_rows"]
    wt = jnp.zeros((LANES, D_MODEL), F32)
    wt = wt.at[ROUTER_GROUP_ROW:ROUTER_GROUP_ROW + MOE_GROUPS].set(w_rg.T)
    wt = wt.at[ROUTER_EXPERT_ROW:ROUTER_EXPERT_ROW + N_EXPERTS].set(w_re.T)
    bias = jnp.zeros((LANES, 1), F32)
    bias = bias.at[ROUTER_GROUP_ROW:ROUTER_GROUP_ROW + MOE_GROUPS, 0].set(b_rg)
    bias = bias.at[ROUTER_EXPERT_ROW:ROUTER_EXPERT_ROW + N_EXPERTS, 0].set(b_re)

    h, eidx, rank, wts, off, tiles = _router(x, norm_ffn.reshape(1, -1), wt, bias, upper, ltri,
                                             t["router_tm"], rows_per_tile)
    pos = _positions(eidx, rank, off).reshape(-1)
    sorted_rows = 2 * s + N_EXPERTS * rows_per_tile
    xs = _dispatch(pos, h, jnp.zeros((sorted_rows, D_MODEL), F32), t["dispatch_tile"])
    ys = _expert_ffn(tiles[0], tiles[1, 0:1], xs, w_gate_e, w_up_e, w_down_e, layer, rows_per_tile)
    return _combine(pos, ys, x, wts.T, t["combine_tile"])


def kernel(x, norm_mix, w_in, conv_w, conv_b, dt_bias, a_log, d_skip, ssd_norm, w_br_ssd, q_norm, k_norm, w_br_att, w_pool, pool_scale, w_br_pool, gate_bias, w_out, norm_ffn, w_router_group, b_router_group, w_router_expert, b_router_expert, w_gate_e, w_up_e, w_down_e):
    b, s, _ = x.shape
    assert b == 1 and s % 1024 == 0 and s % GRID_W == 0
    t = _tiles(s)
    assert 2 * s // t["moe_rows"] + N_EXPERTS <= TILE_TABLE_LANES
    cos, sin = _rope_tables(s)
    bd, perm = _head_constants()
    e_f, e_b = _expand_matrix(0), _expand_matrix(SSD_HEADS)
    rt = t["router_tm"]
    upper = (jnp.arange(rt)[:, None] < jnp.arange(rt)[None, :]).astype(BF16)
    ltri = (jnp.arange(N_EXPERTS)[None, :] < jnp.arange(N_EXPERTS)[:, None]).astype(BF16)

    xx = x[0]
    for l in range(DEPTH):
        xx = _mixer_layer(xx, t, cos, sin, bd, perm, e_f, e_b, norm_mix[l], w_in[l], conv_w[l], conv_b[l], dt_bias[l],
                          a_log[l], d_skip[l], ssd_norm[l], w_br_ssd[l], q_norm[l], k_norm[l], w_br_att[l],
                          w_pool[l], pool_scale[l], w_br_pool[l], gate_bias[l], w_out[l])
        xx = _moe_layer(xx, t, upper, ltri, l, norm_ffn[l], w_router_group[l], b_router_group[l],
                        w_router_expert[l], b_router_expert[l], w_gate_e, w_up_e, w_down_e)
    return xx[None]
```
